```python
import jax, jax.numpy as jnp
from jax import lax
import numpy as np

D_MODEL = 1024
BATCH = 8
SEQ = 2048
DEPTH = 4
DEC_BATCH = 128
DEC_SEQ = 1
PAST_LEN = 16384
PAGE_SIZE = 128

E_CONV = D_MODEL
E_POOL = D_MODEL
CONV_WIDTH = 31
CONV_BUF = CONV_WIDTH - 1
POOL_WINDOWS = (2, 4, 8, 16)
N_POOL_GROUPS = len(POOL_WINDOWS)
POOL_GROUP = E_POOL // N_POOL_GROUPS
POOL_BUF = max(POOL_WINDOWS) - 1
N_BRANCH = 2
D_IN = 3 * E_CONV + 2 * E_POOL + N_BRANCH * D_MODEL
RMS_EPS = 1e-6
LN_EPS = 1e-5

kernel_name = "conv_pool_gated_hybrid_step"


def rms_norm(x, g):
    xf = x.astype(jnp.float32)
    y = xf * lax.rsqrt(jnp.mean(xf * xf, axis=-1, keepdims=True) + RMS_EPS)
    return (y * g.astype(jnp.float32)).astype(x.dtype)


def layer_norm(x, g, b):
    xf = x.astype(jnp.float32)
    mu = jnp.mean(xf, axis=-1, keepdims=True)
    xc = xf - mu
    var = jnp.mean(xc * xc, axis=-1, keepdims=True)
    y = xc * lax.rsqrt(var + LN_EPS) * g.astype(jnp.float32) + b.astype(jnp.float32)
    return y.astype(x.dtype)


def conv_branch(v, gv, z, buf, w_dw, b_dw, ln_g, ln_b, w_a_out):
    u = v * jax.nn.sigmoid(gv)
    ext = jnp.concatenate([buf.astype(u.dtype), u], axis=1)
    y = lax.conv_general_dilated(ext, w_dw[:, None, :].astype(u.dtype), (1,), "VALID",
                                 dimension_numbers=("NWC", "WIO", "NWC"),
                                 feature_group_count=E_CONV) + b_dw
    y = jax.nn.silu(layer_norm(y, ln_g, ln_b))
    y = y * jax.nn.silu(z)
    return y @ w_a_out, ext[:, -CONV_BUF:]


def pool_branch(u, z, buf, start, w_grp, p_scale, w_b_out):
    B, T, _ = u.shape
    ext = jnp.concatenate([buf.astype(u.dtype), u], axis=1)
    cs = jnp.cumsum(ext.astype(jnp.float32), axis=1)
    cs = jnp.pad(cs, ((0, 0), (1, 0), (0, 0)))
    hi = cs[:, POOL_BUF + 1:POOL_BUF + 1 + T]
    pos = start + jnp.arange(T, dtype=jnp.int32)
    means = []
    for k, w in enumerate(POOL_WINDOWS):
        sl = slice(k * POOL_GROUP, (k + 1) * POOL_GROUP)
        lo = cs[:, POOL_BUF + 1 - w:POOL_BUF + 1 - w + T, sl]
        cnt = jnp.minimum(pos + 1, w).astype(jnp.float32)[None, :, None]
        means.append((hi[..., sl] - lo) / cnt)
    p = (jnp.concatenate(means, axis=-1) - u.astype(jnp.float32)).astype(u.dtype)
    p = p.reshape(B, T, N_POOL_GROUPS, POOL_GROUP)
    y = jnp.einsum("btgc,gcd->btgd", p, w_grp).reshape(B, T, E_POOL) * p_scale
    y = y * jax.nn.silu(z)
    return y @ w_b_out, ext[:, -POOL_BUF:]


def trunk(x, c, conv_bufs, pool_bufs, start, w_ada, b_ada, g_norm, w_in, w_dw, b_dw,
          ln_g, ln_b, w_a_out, w_grp, pool_scale, w_b_out, w_out, g_final):
    new_conv, new_pool = [], []
    c_act = jax.nn.silu(c)
    o1, o2, o3 = E_CONV, 2 * E_CONV, 3 * E_CONV
    o4, o5 = o3 + E_POOL, o3 + 2 * E_POOL
    o6 = o5 + D_MODEL
    for l in range(DEPTH):
        mod = (c_act @ w_ada[l] + b_ada[l])[:, None, :]
        shift, scale, gate = jnp.split(mod, 3, axis=-1)
        h = rms_norm(x, g_norm[l]) * (1 + scale) + shift
        proj = h @ w_in[l]
        y_a, nc = conv_branch(proj[..., :o1], proj[..., o1:o2], proj[..., o2:o3], conv_bufs[l],
                              w_dw[l], b_dw[l], ln_g[l], ln_b[l], w_a_out[l])
        y_b, npl = pool_branch(proj[..., o3:o4], proj[..., o4:o5], pool_bufs[l], start,
                               w_grp[l], pool_scale[l], w_b_out[l])
        m = jax.nn.sigmoid(proj[..., o5:o6]) * y_a + jax.nn.sigmoid(proj[..., o6:]) * y_b
        x = x + gate * (m @ w_out[l])
        new_conv.append(nc)
        new_pool.append(npl)
    return rms_norm(x, g_final), jnp.stack(new_conv), jnp.stack(new_pool)


def setup_inputs(seed: int = 0) -> dict:
    key = jax.random.key(seed)
    ks = jax.random.split(key, 24)
    f32 = jnp.float32
    nrm = lambda k, s, sc: jax.random.normal(k, s, f32) * sc
    return {
        "x_prompt": nrm(ks[0], (BATCH, SEQ, D_MODEL), 1.0),
        "x_sample": nrm(ks[1], (DEC_BATCH, DEC_SEQ, D_MODEL), 1.0),
        "state_conv": nrm(ks[2], (DEPTH, DEC_BATCH, CONV_BUF, E_CONV), 1.0),
        "state_pool": nrm(ks[3], (DEPTH, DEC_BATCH, POOL_BUF, E_POOL), 1.0),
        "c_prompt": nrm(ks[4], (BATCH, D_MODEL), 1.0),
        "c_sample": nrm(ks[5], (DEC_BATCH, D_MODEL), 1.0),
        "w_ada": nrm(ks[6], (DEPTH, D_MODEL, 3 * D_MODEL), 0.5 * D_MODEL ** -0.5),
        "b_ada": nrm(ks[7], (DEPTH, 3 * D_MODEL), 0.01),
        "g_norm": 1.0 + nrm(ks[8], (DEPTH, D_MODEL), 0.02),
        "w_in": nrm(ks[9], (DEPTH, D_MODEL, D_IN), D_MODEL ** -0.5),
        "w_dw": nrm(ks[10], (DEPTH, CONV_WIDTH, E_CONV), CONV_WIDTH ** -0.5),
        "b_dw": nrm(ks[11], (DEPTH, E_CONV), 0.01),
        "ln_g": 1.0 + nrm(ks[12], (DEPTH, E_CONV), 0.02),
        "ln_b": nrm(ks[13], (DEPTH, E_CONV), 0.01),
        "w_a_out": nrm(ks[14], (DEPTH, E_CONV, D_MODEL), E_CONV ** -0.5),
        "w_grp": nrm(ks[15], (DEPTH, N_POOL_GROUPS, POOL_GROUP, POOL_GROUP), POOL_GROUP ** -0.5),
        "pool_scale": 1.0 + nrm(ks[16], (DEPTH, E_POOL), 0.1),
        "w_b_out": nrm(ks[17], (DEPTH, E_POOL, D_MODEL), E_POOL ** -0.5),
        "w_out": nrm(ks[18], (DEPTH, D_MODEL, D_MODEL), D_MODEL ** -0.5),
        "g_final": 1.0 + nrm(ks[19], (D_MODEL,), 0.02),
    }


def reference(x_prompt, x_sample, state_conv, state_pool, c_prompt, c_sample, w_ada, b_ada,
              g_norm, w_in, w_dw, b_dw, ln_g, ln_b, w_a_out, w_grp, pool_scale, w_b_out,
              w_out, g_final):
    params = (w_ada, b_ada, g_norm, w_in, w_dw, b_dw, ln_g, ln_b, w_a_out, w_grp,
              pool_scale, w_b_out, w_out, g_final)
    zc = jnp.zeros((DEPTH, BATCH, CONV_BUF, E_CONV), x_prompt.dtype)
    zp = jnp.zeros((DEPTH, BATCH, POOL_BUF, E_POOL), x_prompt.dtype)
    y_prompt, conv_prompt_new, pool_prompt_new = trunk(x_prompt, c_prompt, zc, zp, 0, *params)
    y_sample, conv_sample_new, pool_sample_new = trunk(x_sample, c_sample, state_conv, state_pool,
                                                       PAST_LEN, *params)
    return (y_prompt, y_sample, conv_prompt_new, conv_sample_new, pool_prompt_new, pool_sample_new)
```

```python
import functools

import jax
import jax.numpy as jnp
from jax import lax
from jax.experimental import pallas as pl
from jax.experimental.pallas import tpu as pltpu

D_MODEL = 1024
DEPTH = 4
CONV_WIDTH = 31
CONV_BUF = CONV_WIDTH - 1
POOL_WINDOWS = (2, 4, 8, 16)
POOL_GROUP = D_MODEL // len(POOL_WINDOWS)
POOL_BUF = max(POOL_WINDOWS) - 1
N_PROJ = 7
RMS_EPS = 1e-6
LN_EPS = 1e-5

SUBLANES = 8
LANES = 128
BF16_ROWS = 16
HALO_C = 32
HALO_P = 16
VMEM_LIMIT_BYTES = 56 * 1024 * 1024

TILE_T = 256
CONV_ROWS = 64

F32 = jnp.float32
BF16 = jnp.bfloat16


def _sigmoid(x):
    return jax.nn.sigmoid(x)


def _silu(x):
    return x * jax.nn.sigmoid(x)


def _rms(x, g):
    ms = jnp.mean(x * x, axis=-1, keepdims=True)
    return x * lax.rsqrt(ms + RMS_EPS) * g


def _layer_norm_silu(y, g, b):
    mu = jnp.mean(y, axis=-1, keepdims=True)
    yc = y - mu
    var = jnp.mean(yc * yc, axis=-1, keepdims=True)
    return _silu(yc * lax.rsqrt(var + LN_EPS) * g + b)


def _mod_kernel(c_ref, w_ref, b_ref, o_ref):
    c = c_ref[...]
    o_ref[...] = (
        jnp.dot(_silu(c).astype(BF16), w_ref[...].astype(BF16), preferred_element_type=F32)
        + b_ref[...]
    )


def _ada_mod(c_all, w_ada, b_ada):
    n = c_all.shape[0]
    return pl.pallas_call(
        _mod_kernel,
        grid=(DEPTH, 3),
        in_specs=[
            pl.BlockSpec((n, D_MODEL), lambda l, j: (0, 0)),
            pl.BlockSpec((None, D_MODEL, D_MODEL), lambda l, j: (l, 0, j)),
            pl.BlockSpec((None, 1, D_MODEL), lambda l, j: (l, 0, j)),
        ],
        out_specs=pl.BlockSpec((None, n, D_MODEL), lambda l, j: (l, 0, j)),
        out_shape=jax.ShapeDtypeStruct((DEPTH, n, 3 * D_MODEL), F32),
        name="ada_mod",
    )(c_all, w_ada, b_ada.reshape(DEPTH, 1, 3 * D_MODEL))


def _prompt_layer_kernel(
    x_ref, mod_ref, gn_ref, win_ref, wdw_ref, bdw_ref, lng_ref, lnb_ref, waout_ref, wgrp_ref,
    psc_ref, wbout_ref, wout_ref, gfin_ref, pwin_ref,
    xo_ref, cnew_ref, pnew_ref,
    extu, extp, h_s, proj_s, y_s, a_s, ya_s, yb_s,
    *, tt, final,
):
    b = pl.program_id(0)
    t = pl.program_id(1)
    mod = mod_ref[pl.ds(b, 1), :]
    shift = mod[:, 0:D_MODEL]
    scale1 = 1.0 + mod[:, D_MODEL:2 * D_MODEL]
    gate = mod[:, 2 * D_MODEL:3 * D_MODEL]

    @pl.when(t == 0)
    def _():
        extu[0:HALO_C, :] = jnp.zeros((HALO_C, D_MODEL), F32)
        extp[0:HALO_P, :] = jnp.zeros((HALO_P, D_MODEL), F32)

    n16 = tt // BF16_ROWS

    def rows16(i):
        return pl.ds(pl.multiple_of(i * BF16_ROWS, BF16_ROWS), BF16_ROWS)

    def h_body(i, c):
        r = rows16(i)
        h = _rms(x_ref[r, :], gn_ref[...]) * scale1 + shift
        h_s[r, :] = h.astype(BF16)
        return c

    lax.fori_loop(0, n16, h_body, 0)

    proj_s[...] = jnp.dot(h_s[...], win_ref[...], preferred_element_type=F32)

    def u_body(i, c):
        r0 = pl.multiple_of(i * BF16_ROWS, BF16_ROWS)
        r = pl.ds(r0, BF16_ROWS)
        v = proj_s[r, 0:D_MODEL]
        gv = proj_s[r, D_MODEL:2 * D_MODEL]
        extu[pl.ds(HALO_C + r0, BF16_ROWS), :] = v * _sigmoid(gv)
        extp[pl.ds(HALO_P + r0, BF16_ROWS), :] = proj_s[r, 3 * D_MODEL:4 * D_MODEL]
        return c

    lax.fori_loop(0, n16, u_body, 0)

    n_lane_tiles = D_MODEL // LANES

    def conv_body(i, c):
        r0 = pl.multiple_of(i * CONV_ROWS, CONV_ROWS)
        win = extu.at[pl.ds(r0, CONV_ROWS + HALO_C), :]
        for j in range(n_lane_tiles):
            lanes = slice(j * LANES, (j + 1) * LANES)
            acc = jnp.zeros((CONV_ROWS, LANES), F32)
            for k in range(CONV_WIDTH):
                k0 = HALO_C - CONV_BUF + k
                acc = acc + wdw_ref[k:k + 1, lanes] * win[k0:k0 + CONV_ROWS, lanes]
            y_s[pl.ds(r0, CONV_ROWS), lanes] = acc + bdw_ref[:, lanes]
        return c

    lax.fori_loop(0, tt // CONV_ROWS, conv_body, 0)

    def a_body(i, c):
        r = rows16(i)
        s = _layer_norm_silu(y_s[r, :], lng_ref[...], lnb_ref[...])
        a_s[r, :] = (s * _silu(proj_s[r, 2 * D_MODEL:3 * D_MODEL])).astype(BF16)
        return c

    lax.fori_loop(0, n16, a_body, 0)
    ya_s[...] = jnp.dot(a_s[...], waout_ref[...], preferred_element_type=F32)

    def p_body(i, c):
        r0 = pl.multiple_of(i * BF16_ROWS, BF16_ROWS)
        win = extp.at[pl.ds(r0, BF16_ROWS + HALO_P), :]
        parts = []
        for g, w in enumerate(POOL_WINDOWS):
            lanes = slice(g * POOL_GROUP, (g + 1) * POOL_GROUP)
            cur = win[HALO_P:HALO_P + BF16_ROWS, lanes]
            s = cur
            for j in range(1, w):
                s = s + win[HALO_P - j:HALO_P - j + BF16_ROWS, lanes]
            parts.append(s * (1.0 / w) - cur)
        a_s[pl.ds(r0, BF16_ROWS), :] = jnp.concatenate(parts, axis=-1).astype(BF16)
        return c

    lax.fori_loop(0, n16, p_body, 0)

    @pl.when(t == 0)
    def _():
        cur = extp[HALO_P:2 * HALO_P, :]
        pos1 = lax.broadcasted_iota(jnp.int32, (HALO_P, D_MODEL), 0).astype(F32) + 1.0
        wv = pwin_ref[...]
        cnt = jnp.minimum(pos1, wv)
        s = jnp.zeros((HALO_P, D_MODEL), F32)
        for j in range(max(POOL_WINDOWS)):
            s = s + jnp.where(wv > j, extp[HALO_P - j:2 * HALO_P - j, :], 0.0)
        a_s[0:HALO_P, :] = (s / cnt - cur).astype(BF16)

    for g in range(len(POOL_WINDOWS)):
        lanes = slice(g * POOL_GROUP, (g + 1) * POOL_GROUP)
        yb_s[:, lanes] = jnp.dot(a_s[:, lanes], wgrp_ref[g], preferred_element_type=F32)

    def b_body(i, c):
        r = rows16(i)
        q = yb_s[r, :] * psc_ref[...] * _silu(proj_s[r, 4 * D_MODEL:5 * D_MODEL])
        a_s[r, :] = q.astype(BF16)
        return c

    lax.fori_loop(0, n16, b_body, 0)
    yb_s[...] = jnp.dot(a_s[...], wbout_ref[...], preferred_element_type=F32)

    def m_body(i, c):
        r = rows16(i)
        m = (_sigmoid(proj_s[r, 5 * D_MODEL:6 * D_MODEL]) * ya_s[r, :]
             + _sigmoid(proj_s[r, 6 * D_MODEL:7 * D_MODEL]) * yb_s[r, :])
        a_s[r, :] = m.astype(BF16)
        return c

    lax.fori_loop(0, n16, m_body, 0)
    ya_s[...] = jnp.dot(a_s[...], wout_ref[...], preferred_element_type=F32)

    def o_body(i, c):
        r = rows16(i)
        xn = x_ref[r, :] + gate * ya_s[r, :]
        if final:
            xn = _rms(xn, gfin_ref[...])
        xo_ref[r, :] = xn
        return c

    lax.fori_loop(0, n16, o_body, 0)

    cnew_ref[...] = extu[HALO_C + tt - CONV_BUF:HALO_C + tt, :]
    pnew_ref[...] = extp[HALO_P + tt - POOL_BUF:HALO_P + tt, :]
    extu[0:HALO_C, :] = extu[tt:tt + HALO_C, :]
    extp[0:HALO_P, :] = extp[tt:tt + HALO_P, :]


def _const_spec(shape):
    nd = len(shape)
    return pl.BlockSpec(shape, lambda b, t: (0,) * nd, pipeline_mode=pl.Buffered(1))


def _layer_spec(l, shape):
    nd = len(shape)
    return pl.BlockSpec((None,) + shape, lambda b, t: (l,) + (0,) * nd, pipeline_mode=pl.Buffered(1))


def _prompt_layer(l, x, mod, n_sample, p, pwin, final):
    bsz, seq, _ = x.shape
    tt = TILE_T
    kern = functools.partial(_prompt_layer_kernel, tt=tt, final=final)
    mod_blk = n_sample // bsz
    in_specs = [
        pl.BlockSpec((None, tt, D_MODEL), lambda b, t: (b, t, 0)),
        pl.BlockSpec((None, bsz, 3 * D_MODEL), lambda b, t: (l, mod_blk, 0), pipeline_mode=pl.Buffered(1)),
        _layer_spec(l, (1, D_MODEL)),
        _layer_spec(l, (D_MODEL, N_PROJ * D_MODEL)),
        _layer_spec(l, (CONV_WIDTH, D_MODEL)),
        _layer_spec(l, (1, D_MODEL)),
        _layer_spec(l, (1, D_MODEL)),
        _layer_spec(l, (1, D_MODEL)),
        _layer_spec(l, (D_MODEL, D_MODEL)),
        _layer_spec(l, (len(POOL_WINDOWS), POOL_GROUP, POOL_GROUP)),
        _layer_spec(l, (1, D_MODEL)),
        _layer_spec(l, (D_MODEL, D_MODEL)),
        _layer_spec(l, (D_MODEL, D_MODEL)),
        _const_spec((1, D_MODEL)),
        _const_spec((1, D_MODEL)),
    ]
    out_specs = [
        pl.BlockSpec((None, tt, D_MODEL), lambda b, t: (b, t, 0)),
        pl.BlockSpec((None, CONV_BUF, D_MODEL), lambda b, t: (b, 0, 0)),
        pl.BlockSpec((None, POOL_BUF, D_MODEL), lambda b, t: (b, 0, 0)),
    ]
    out_shape = [
        jax.ShapeDtypeStruct((bsz, seq, D_MODEL), F32),
        jax.ShapeDtypeStruct((bsz, CONV_BUF, D_MODEL), F32),
        jax.ShapeDtypeStruct((bsz, POOL_BUF, D_MODEL), F32),
    ]
    scratch = [
        pltpu.VMEM((HALO_C + tt, D_MODEL), F32),
        pltpu.VMEM((HALO_P + tt, D_MODEL), F32),
        pltpu.VMEM((tt, D_MODEL), BF16),
        pltpu.VMEM((tt, N_PROJ * D_MODEL), F32),
        pltpu.VMEM((tt, D_MODEL), F32),
        pltpu.VMEM((tt, D_MODEL), BF16),
        pltpu.VMEM((tt, D_MODEL), F32),
        pltpu.VMEM((tt, D_MODEL), F32),
    ]
    return pl.pallas_call(
        kern,
        grid=(bsz, seq // tt),
        in_specs=in_specs,
        out_specs=out_specs,
        out_shape=out_shape,
        scratch_shapes=scratch,
        compiler_params=pltpu.CompilerParams(
            dimension_semantics=("arbitrary", "arbitrary"), vmem_limit_bytes=VMEM_LIMIT_BYTES),
        name=f"prompt_layer{l}",
    )(x, mod, p["g_norm"], p["w_in"], p["w_dw"], p["b_dw"], p["ln_g"], p["ln_b"], p["w_a_out"],
      p["w_grp"], p["pool_scale"], p["w_b_out"], p["w_out"], p["g_final"], pwin)


SAMPLE_ROWS = 16


def _state_reduce_kernel(sc_ref, sp_ref, wdw_ref, pmask_ref, s_ref, ps_ref):
    s_ref[...] = jnp.sum(sc_ref[...] * wdw_ref[0:CONV_BUF, :][None], axis=1)
    ps_ref[...] = jnp.sum(sp_ref[...] * pmask_ref[...][None], axis=1)


def _state_reduce(state_conv, state_pool, w_dw, pmask):
    nb = state_conv.shape[1]
    rb = SAMPLE_ROWS
    return pl.pallas_call(
        _state_reduce_kernel,
        grid=(DEPTH, nb // rb),
        in_specs=[
            pl.BlockSpec((None, rb, CONV_BUF, D_MODEL), lambda l, i: (l, i, 0, 0)),
            pl.BlockSpec((None, rb, POOL_BUF, D_MODEL), lambda l, i: (l, i, 0, 0)),
            pl.BlockSpec((None, CONV_WIDTH, D_MODEL), lambda l, i: (l, 0, 0)),
            pl.BlockSpec((POOL_BUF, D_MODEL), lambda l, i: (0, 0)),
        ],
        out_specs=[
            pl.BlockSpec((None, rb, D_MODEL), lambda l, i: (l, i, 0)),
            pl.BlockSpec((None, rb, D_MODEL), lambda l, i: (l, i, 0)),
        ],
        out_shape=[jax.ShapeDtypeStruct((DEPTH, nb, D_MODEL), F32)] * 2,
        name="sample_state_reduce",
    )(state_conv, state_pool, w_dw, pmask)


def _sample_trunk_kernel(
    x_ref, mod_ref, gn_ref, win_ref, wdw_ref, bdw_ref, lng_ref, lnb_ref, waout_ref, wgrp_ref,
    psc_ref, wbout_ref, wout_ref, gfin_ref, pwinv_ref, s_ref, ps_ref,
    y_ref, u_ref, pu_ref,
    xs,
):
    l = pl.program_id(0)

    @pl.when(l == 0)
    def _():
        xs[...] = x_ref[...]

    x = xs[...]
    mod = mod_ref[...]
    shift = mod[:, 0:D_MODEL]
    scale = mod[:, D_MODEL:2 * D_MODEL]
    gate = mod[:, 2 * D_MODEL:3 * D_MODEL]
    h = (_rms(x, gn_ref[...]) * (1.0 + scale) + shift).astype(BF16)
    proj = jnp.dot(h, win_ref[...], preferred_element_type=F32)

    def col(j):
        return proj[:, j * D_MODEL:(j + 1) * D_MODEL]

    u = col(0) * _sigmoid(col(1))
    u_ref[...] = u
    y = s_ref[...] + wdw_ref[CONV_BUF:CONV_WIDTH, :] * u + bdw_ref[...]
    a = _layer_norm_silu(y, lng_ref[...], lnb_ref[...]) * _silu(col(2))
    y_a = jnp.dot(a.astype(BF16), waout_ref[...], preferred_element_type=F32)

    pu = col(3)
    pu_ref[...] = pu
    pm = ((ps_ref[...] + pu) * pwinv_ref[...] - pu).astype(BF16)
    q = jnp.concatenate(
        [jnp.dot(pm[:, g * POOL_GROUP:(g + 1) * POOL_GROUP], wgrp_ref[g], preferred_element_type=F32)
         for g in range(len(POOL_WINDOWS))], axis=-1)
    bq = q * psc_ref[...] * _silu(col(4))
    y_b = jnp.dot(bq.astype(BF16), wbout_ref[...], preferred_element_type=F32)

    m = _sigmoid(col(5)) * y_a + _sigmoid(col(6)) * y_b
    xn = x + gate * jnp.dot(m.astype(BF16), wout_ref[...], preferred_element_type=F32)
    xs[...] = xn

    @pl.when(l == DEPTH - 1)
    def _():
        y_ref[...] = _rms(xn, gfin_ref[...])


def _sample_trunk(x, mod, p, pwinv, s_all, ps_all):
    nb = x.shape[0]

    def lspec(shape):
        nd = len(shape)
        return pl.BlockSpec((None,) + shape, lambda l: (l,) + (0,) * nd)

    def cspec(shape):
        nd = len(shape)
        return pl.BlockSpec(shape, lambda l: (0,) * nd)

    in_specs = [
        cspec((nb, D_MODEL)),
        lspec((nb, 3 * D_MODEL)),
        lspec((1, D_MODEL)),
        lspec((D_MODEL, N_PROJ * D_MODEL)),
        lspec((CONV_WIDTH, D_MODEL)),
        lspec((1, D_MODEL)),
        lspec((1, D_MODEL)),
        lspec((1, D_MODEL)),
        lspec((D_MODEL, D_MODEL)),
        lspec((len(POOL_WINDOWS), POOL_GROUP, POOL_GROUP)),
        lspec((1, D_MODEL)),
        lspec((D_MODEL, D_MODEL)),
        lspec((D_MODEL, D_MODEL)),
        cspec((1, D_MODEL)),
        cspec((1, D_MODEL)),
        lspec((nb, D_MODEL)),
        lspec((nb, D_MODEL)),
    ]
    out_specs = [cspec((nb, D_MODEL)), lspec((nb, D_MODEL)), lspec((nb, D_MODEL))]
    out_shape = [
        jax.ShapeDtypeStruct((nb, D_MODEL), F32),
        jax.ShapeDtypeStruct((DEPTH, nb, D_MODEL), F32),
        jax.ShapeDtypeStruct((DEPTH, nb, D_MODEL), F32),
    ]
    return pl.pallas_call(
        _sample_trunk_kernel,
        grid=(DEPTH,),
        in_specs=in_specs,
        out_specs=out_specs,
        out_shape=out_shape,
        scratch_shapes=[pltpu.VMEM((nb, D_MODEL), F32)],
        compiler_params=pltpu.CompilerParams(
            dimension_semantics=("arbitrary",), vmem_limit_bytes=VMEM_LIMIT_BYTES),
        name="sample_trunk",
    )(x, mod, p["g_norm"], p["w_in"], p["w_dw"], p["b_dw"], p["ln_g"], p["ln_b"], p["w_a_out"],
      p["w_grp"], p["pool_scale"], p["w_b_out"], p["w_out"], p["g_final"], pwinv, s_all, ps_all)


def _state_shift_kernel(s_ref, new_ref, o_ref, *, nbuf):
    for i in range(SAMPLE_ROWS):
        o_ref[i, 0:nbuf - 1, :] = s_ref[i, 1:nbuf, :]
        o_ref[i, nbuf - 1:nbuf, :] = new_ref[i:i + 1, :]


def _state_shift(state, new_rows, name):
    _, nb, nbuf, _ = state.shape
    rb = SAMPLE_ROWS
    return pl.pallas_call(
        functools.partial(_state_shift_kernel, nbuf=nbuf),
        grid=(DEPTH, nb // rb),
        in_specs=[
            pl.BlockSpec((None, rb, nbuf, D_MODEL), lambda l, i: (l, i, 0, 0)),
            pl.BlockSpec((None, rb, D_MODEL), lambda l, i: (l, i, 0)),
        ],
        out_specs=pl.BlockSpec((None, rb, nbuf, D_MODEL), lambda l, i: (l, i, 0, 0)),
        out_shape=jax.ShapeDtypeStruct(state.shape, F32),
        name=name,
    )(state, new_rows)


def kernel(x_prompt, x_sample, state_conv, state_pool, c_prompt, c_sample, w_ada, b_ada, g_norm, w_in, w_dw, b_dw, ln_g, ln_b, w_a_out, w_grp, pool_scale, w_b_out, w_out, g_final):
    n_sample = x_sample.shape[0]
    row = lambda a: a.reshape(DEPTH, 1, D_MODEL)
    p = dict(
        g_norm=row(g_norm), w_in=w_in.astype(BF16), w_dw=w_dw, b_dw=row(b_dw), ln_g=row(ln_g),
        ln_b=row(ln_b), w_a_out=w_a_out.astype(BF16), w_grp=w_grp.astype(BF16),
        pool_scale=row(pool_scale), w_b_out=w_b_out.astype(BF16), w_out=w_out.astype(BF16),
        g_final=g_final.reshape(1, D_MODEL),
    )
    win_lane = jnp.repeat(jnp.asarray(POOL_WINDOWS, F32), POOL_GROUP).reshape(1, D_MODEL)
    pmask = (jnp.arange(POOL_BUF, dtype=F32)[:, None] >= (POOL_BUF + 1) - win_lane).astype(F32)

    mod = _ada_mod(jnp.concatenate([c_sample, c_prompt], axis=0), w_ada, b_ada)

    x = x_prompt
    conv_p, pool_p = [], []
    for l in range(DEPTH):
        x, cn, pn = _prompt_layer(l, x, mod, n_sample, p, win_lane, final=(l == DEPTH - 1))
        conv_p.append(cn)
        pool_p.append(pn)
    y_prompt = x

    s_all, ps_all = _state_reduce(state_conv, state_pool, w_dw, pmask)
    y_s, u_all, pu_all = _sample_trunk(x_sample[:, 0, :], mod, p, 1.0 / win_lane, s_all, ps_all)
    conv_s = _state_shift(state_conv, u_all, "sample_conv_state")
    pool_s = _state_shift(state_pool, pu_all, "sample_pool_state")

    return (y_prompt, y_s[:, None, :], jnp.stack(conv_p), conv_s, jnp.stack(pool_p), pool_s)
```

```python
import functools

import jax
import jax.numpy as jnp
from jax import lax
from jax.experimental import pallas as pl
from jax.experimental.pallas import tpu as pltpu

D_MODEL = 1024
DEPTH = 4
CONV_WIDTH = 31
CONV_BUF = CONV_WIDTH - 1
POOL_WINDOWS = (2, 4, 8, 16)
POOL_GROUP = D_MODEL // len(POOL_WINDOWS)
POOL_BUF = max(POOL_WINDOWS) - 1
N_PROJ = 7
RMS_EPS = 1e-6
LN_EPS = 1e-5

SUBLANES = 8
LANES = 128
HALO_C = 32
HALO_P = 16
VMEM_LIMIT_BYTES = 56 * 1024 * 1024

TILE_T = 256
N_SLABS = D_MODEL // LANES
PHASES = 2
GROUP = SUBLANES * PHASES

F32 = jnp.float32
BF16 = jnp.bfloat16


def _sigmoid(x):
    return jax.nn.sigmoid(x)


def _silu(x):
    return x * jax.nn.sigmoid(x)


def _rms(x, g):
    ms = jnp.mean(x * x, axis=-1, keepdims=True)
    return x * lax.rsqrt(ms + RMS_EPS) * g


def _layer_norm_silu(y, g, b):
    mu = jnp.mean(y, axis=-1, keepdims=True)
    yc = y - mu
    var = jnp.mean(yc * yc, axis=-1, keepdims=True)
    return _silu(yc * lax.rsqrt(var + LN_EPS) * g + b)


def _mod_kernel(c_ref, w_ref, b_ref, o_ref):
    c = c_ref[...]
    o_ref[...] = (
        jnp.dot(_silu(c).astype(BF16), w_ref[...].astype(BF16), preferred_element_type=F32)
        + b_ref[...]
    )


def _ada_mod(c_all, w_ada, b_ada):
    n = c_all.shape[0]
    return pl.pallas_call(
        _mod_kernel,
        grid=(DEPTH, 3),
        in_specs=[
            pl.BlockSpec((n, D_MODEL), lambda l, j: (0, 0)),
            pl.BlockSpec((None, D_MODEL, D_MODEL), lambda l, j: (l, 0, j)),
            pl.BlockSpec((None, 1, D_MODEL), lambda l, j: (l, 0, j)),
        ],
        out_specs=pl.BlockSpec((None, n, D_MODEL), lambda l, j: (l, 0, j)),
        out_shape=jax.ShapeDtypeStruct((DEPTH, n, 3 * D_MODEL), F32),
        name="ada_mod",
    )(c_all, w_ada, b_ada.reshape(DEPTH, 1, 3 * D_MODEL))


def _prompt_layer_kernel(
    x_ref, mod_ref, gn_ref, win_ref, wdw_ref, bdw_ref, lng_ref, lnb_ref, waout_ref, wgrp_ref,
    psc_ref, wbout_ref, wout_ref, gfin_ref,
    xo_ref, cnew_ref, pnew_ref,
    extu, extp, h_s, proj_s, y_s, q_s, a_s, p_s, b_s, m_s, ya_s, yb_s, o_s,
    *, tt, final,
):
    b = pl.program_id(0)
    t = pl.program_id(1)
    mod = mod_ref[pl.ds(b, 1), :]
    shift = mod[:, 0:D_MODEL]
    gain = gn_ref[...] * (1.0 + mod[:, D_MODEL:2 * D_MODEL])
    gate = mod[:, 2 * D_MODEL:3 * D_MODEL]

    @pl.when(t == 0)
    def _():
        for j in range(N_SLABS):
            extu[j, 0:HALO_C, :] = jnp.zeros((HALO_C, LANES), F32)
            extp[j, 0:HALO_P, :] = jnp.zeros((HALO_P, LANES), F32)

    n_groups = tt // GROUP

    def rows(g):
        return slice(g * GROUP, (g + 1) * GROUP)

    def lanes(j):
        return slice(j * LANES, (j + 1) * LANES)

    def col(blk):
        return slice(blk * D_MODEL, (blk + 1) * D_MODEL)

    def phase_rows(r0, ph):
        return pl.ds(r0 + ph, SUBLANES, stride=PHASES)

    for g in range(n_groups):
        x = x_ref[rows(g), :]
        ms = jnp.mean(x * x, axis=-1, keepdims=True)
        h_s[rows(g), :] = (x * lax.rsqrt(ms + RMS_EPS) * gain + shift).astype(BF16)

    for blk in (0, 1, 3, 2, 4, 5, 6):
        proj_s[:, col(blk)] = jnp.dot(h_s[...], win_ref[:, col(blk)], preferred_element_type=F32)

    for g in range(n_groups):
        u = proj_s[rows(g), col(0)] * _sigmoid(proj_s[rows(g), col(1)])
        pu = proj_s[rows(g), col(3)]
        for j in range(N_SLABS):
            extu[j, HALO_C + g * GROUP:HALO_C + (g + 1) * GROUP, :] = u[:, lanes(j)]
            extp[j, HALO_P + g * GROUP:HALO_P + (g + 1) * GROUP, :] = pu[:, lanes(j)]

    for j in range(N_SLABS):
        taps = [jnp.broadcast_to(wdw_ref[k:k + 1, lanes(j)], (SUBLANES, LANES)) for k in range(CONV_WIDTH)]
        bias = jnp.broadcast_to(bdw_ref[:, lanes(j)], (SUBLANES, LANES))
        for g in range(n_groups):
            for ph in range(PHASES):
                acc = bias
                for k in range(CONV_WIDTH):
                    acc = acc + taps[k] * extu[j, phase_rows(g * GROUP + HALO_C - CONV_BUF + k, ph), :]
                y_s[j, phase_rows(g * GROUP, ph), :] = acc

    for g in range(n_groups):
        y = jnp.concatenate([y_s[j, rows(g), :] for j in range(N_SLABS)], axis=-1)
        s = _layer_norm_silu(y, lng_ref[...], lnb_ref[...])
        a_s[rows(g), :] = (s * _silu(proj_s[rows(g), col(2)])).astype(BF16)

    ya_s[...] = jnp.dot(a_s[...], waout_ref[...], preferred_element_type=F32)

    for j in range(N_SLABS):
        w = POOL_WINDOWS[j * LANES // POOL_GROUP]
        for g in range(n_groups):
            for ph in range(PHASES):
                cur = extp[j, phase_rows(g * GROUP + HALO_P, ph), :]
                s = cur
                for i in range(1, w):
                    s = s + extp[j, phase_rows(g * GROUP + HALO_P - i, ph), :]
                q_s[j, phase_rows(g * GROUP, ph), :] = s * (1.0 / w) - cur

    @pl.when(t == 0)
    def _():
        pos1 = lax.broadcasted_iota(jnp.int32, (HALO_P, LANES), 0).astype(F32) + 1.0
        for j in range(N_SLABS):
            w = POOL_WINDOWS[j * LANES // POOL_GROUP]
            cur = extp[j, HALO_P:2 * HALO_P, :]
            s = cur
            for i in range(1, w):
                s = s + extp[j, HALO_P - i:2 * HALO_P - i, :]
            q_s[j, 0:HALO_P, :] = s / jnp.minimum(pos1, float(w)) - cur

    for g in range(n_groups):
        p = jnp.concatenate([q_s[j, rows(g), :] for j in range(N_SLABS)], axis=-1)
        p_s[rows(g), :] = p.astype(BF16)

    for gi in range(len(POOL_WINDOWS)):
        gl = slice(gi * POOL_GROUP, (gi + 1) * POOL_GROUP)
        yb_s[:, gl] = jnp.dot(p_s[:, gl], wgrp_ref[gi], preferred_element_type=F32)

    for g in range(n_groups):
        q = yb_s[rows(g), :] * psc_ref[...] * _silu(proj_s[rows(g), col(4)])
        b_s[rows(g), :] = q.astype(BF16)

    o_s[...] = jnp.dot(b_s[...], wbout_ref[...], preferred_element_type=F32)

    for g in range(n_groups):
        m = (_sigmoid(proj_s[rows(g), col(5)]) * ya_s[rows(g), :]
             + _sigmoid(proj_s[rows(g), col(6)]) * o_s[rows(g), :])
        m_s[rows(g), :] = m.astype(BF16)

    yb_s[...] = jnp.dot(m_s[...], wout_ref[...], preferred_element_type=F32)

    for g in range(n_groups):
        xn = x_ref[rows(g), :] + gate * yb_s[rows(g), :]
        if final:
            xn = _rms(xn, gfin_ref[...])
        xo_ref[rows(g), :] = xn

    for j in range(N_SLABS):
        cnew_ref[:, lanes(j)] = extu[j, HALO_C + tt - CONV_BUF:HALO_C + tt, :]
        pnew_ref[:, lanes(j)] = extp[j, HALO_P + tt - POOL_BUF:HALO_P + tt, :]
        extu[j, 0:HALO_C, :] = extu[j, tt:tt + HALO_C, :]
        extp[j, 0:HALO_P, :] = extp[j, tt:tt + HALO_P, :]


def _const_spec(shape):
    nd = len(shape)
    return pl.BlockSpec(shape, lambda b, t: (0,) * nd, pipeline_mode=pl.Buffered(1))


def _layer_spec(l, shape):
    nd = len(shape)
    return pl.BlockSpec((None,) + shape, lambda b, t: (l,) + (0,) * nd, pipeline_mode=pl.Buffered(1))


def _prompt_layer(l, x, mod, n_sample, p, final):
    bsz, seq, _ = x.shape
    tt = TILE_T
    kern = functools.partial(_prompt_layer_kernel, tt=tt, final=final)
    mod_blk = n_sample // bsz
    in_specs = [
        pl.BlockSpec((None, tt, D_MODEL), lambda b, t: (b, t, 0)),
        pl.BlockSpec((None, bsz, 3 * D_MODEL), lambda b, t: (l, mod_blk, 0), pipeline_mode=pl.Buffered(1)),
        _layer_spec(l, (1, D_MODEL)),
        _layer_spec(l, (D_MODEL, N_PROJ * D_MODEL)),
        _layer_spec(l, (CONV_WIDTH, D_MODEL)),
        _layer_spec(l, (1, D_MODEL)),
        _layer_spec(l, (1, D_MODEL)),
        _layer_spec(l, (1, D_MODEL)),
        _layer_spec(l, (D_MODEL, D_MODEL)),
        _layer_spec(l, (len(POOL_WINDOWS), POOL_GROUP, POOL_GROUP)),
        _layer_spec(l, (1, D_MODEL)),
        _layer_spec(l, (D_MODEL, D_MODEL)),
        _layer_spec(l, (D_MODEL, D_MODEL)),
        _const_spec((1, D_MODEL)),
    ]
    out_specs = [
        pl.BlockSpec((None, tt, D_MODEL), lambda b, t: (b, t, 0)),
        pl.BlockSpec((None, CONV_BUF, D_MODEL), lambda b, t: (b, 0, 0)),
        pl.BlockSpec((None, POOL_BUF, D_MODEL), lambda b, t: (b, 0, 0)),
    ]
    out_shape = [
        jax.ShapeDtypeStruct((bsz, seq, D_MODEL), F32),
        jax.ShapeDtypeStruct((bsz, CONV_BUF, D_MODEL), F32),
        jax.ShapeDtypeStruct((bsz, POOL_BUF, D_MODEL), F32),
    ]
    act_bf16 = pltpu.VMEM((tt, D_MODEL), BF16)
    act_f32 = pltpu.VMEM((tt, D_MODEL), F32)
    slabs = pltpu.VMEM((N_SLABS, tt, LANES), F32)
    scratch = [
        pltpu.VMEM((N_SLABS, HALO_C + tt, LANES), F32),
        pltpu.VMEM((N_SLABS, HALO_P + tt, LANES), F32),
        act_bf16,
        pltpu.VMEM((tt, N_PROJ * D_MODEL), F32),
        slabs, slabs,
        act_bf16, act_bf16, act_bf16, act_bf16,
        act_f32, act_f32, act_f32,
    ]
    return pl.pallas_call(
        kern,
        grid=(bsz, seq // tt),
        in_specs=in_specs,
        out_specs=out_specs,
        out_shape=out_shape,
        scratch_shapes=scratch,
        compiler_params=pltpu.CompilerParams(
            dimension_semantics=("arbitrary", "arbitrary"), vmem_limit_bytes=VMEM_LIMIT_BYTES),
        name=f"prompt_layer{l}",
    )(x, mod, p["g_norm"], p["w_in"], p["w_dw"], p["b_dw"], p["ln_g"], p["ln_b"], p["w_a_out"],
      p["w_grp"], p["pool_scale"], p["w_b_out"], p["w_out"], p["g_final"])


SAMPLE_ROWS = 16


def _state_reduce_kernel(sc_ref, sp_ref, wdw_ref, pmask_ref, s_ref, ps_ref):
    s_ref[...] = jnp.sum(sc_ref[...] * wdw_ref[0:CONV_BUF, :][None], axis=1)
    ps_ref[...] = jnp.sum(sp_ref[...] * pmask_ref[...][None], axis=1)


def _state_reduce(state_conv, state_pool, w_dw, pmask):
    nb = state_conv.shape[1]
    rb = SAMPLE_ROWS
    return pl.pallas_call(
        _state_reduce_kernel,
        grid=(DEPTH, nb // rb),
        in_specs=[
            pl.BlockSpec((None, rb, CONV_BUF, D_MODEL), lambda l, i: (l, i, 0, 0)),
            pl.BlockSpec((None, rb, POOL_BUF, D_MODEL), lambda l, i: (l, i, 0, 0)),
            pl.BlockSpec((None, CONV_WIDTH, D_MODEL), lambda l, i: (l, 0, 0)),
            pl.BlockSpec((POOL_BUF, D_MODEL), lambda l, i: (0, 0)),
        ],
        out_specs=[
            pl.BlockSpec((None, rb, D_MODEL), lambda l, i: (l, i, 0)),
            pl.BlockSpec((None, rb, D_MODEL), lambda l, i: (l, i, 0)),
        ],
        out_shape=[jax.ShapeDtypeStruct((DEPTH, nb, D_MODEL), F32)] * 2,
        name="sample_state_reduce",
    )(state_conv, state_pool, w_dw, pmask)


def _sample_trunk_kernel(
    x_ref, mod_ref, gn_ref, win_ref, wdw_ref, bdw_ref, lng_ref, lnb_ref, waout_ref, wgrp_ref,
    psc_ref, wbout_ref, wout_ref, gfin_ref, pwinv_ref, s_ref, ps_ref,
    y_ref, u_ref, pu_ref,
    xs,
):
    l = pl.program_id(0)

    @pl.when(l == 0)
    def _():
        xs[...] = x_ref[...]

    x = xs[...]
    mod = mod_ref[...]
    shift = mod[:, 0:D_MODEL]
    scale = mod[:, D_MODEL:2 * D_MODEL]
    gate = mod[:, 2 * D_MODEL:3 * D_MODEL]
    h = (_rms(x, gn_ref[...]) * (1.0 + scale) + shift).astype(BF16)
    proj = jnp.dot(h, win_ref[...], preferred_element_type=F32)

    def col(j):
        return proj[:, j * D_MODEL:(j + 1) * D_MODEL]

    u = col(0) * _sigmoid(col(1))
    u_ref[...] = u
    y = s_ref[...] + wdw_ref[CONV_BUF:CONV_WIDTH, :] * u + bdw_ref[...]
    a = _layer_norm_silu(y, lng_ref[...], lnb_ref[...]) * _silu(col(2))
    y_a = jnp.dot(a.astype(BF16), waout_ref[...], preferred_element_type=F32)

    pu = col(3)
    pu_ref[...] = pu
    pm = ((ps_ref[...] + pu) * pwinv_ref[...] - pu).astype(BF16)
    q = jnp.concatenate(
        [jnp.dot(pm[:, g * POOL_GROUP:(g + 1) * POOL_GROUP], wgrp_ref[g], preferred_element_type=F32)
         for g in range(len(POOL_WINDOWS))], axis=-1)
    bq = q * psc_ref[...] * _silu(col(4))
    y_b = jnp.dot(bq.astype(BF16), wbout_ref[...], preferred_element_type=F32)

    m = _sigmoid(col(5)) * y_a + _sigmoid(col(6)) * y_b
    xn = x + gate * jnp.dot(m.astype(BF16), wout_ref[...], preferred_element_type=F32)
    xs[...] = xn

    @pl.when(l == DEPTH - 1)
    def _():
        y_ref[...] = _rms(xn, gfin_ref[...])


def _sample_trunk(x, mod, p, pwinv, s_all, ps_all):
    nb = x.shape[0]

    def lspec(shape):
        nd = len(shape)
        return pl.BlockSpec((None,) + shape, lambda l: (l,) + (0,) * nd)

    def cspec(shape):
        nd = len(shape)
        return pl.BlockSpec(shape, lambda l: (0,) * nd)

    in_specs = [
        cspec((nb, D_MODEL)),
        lspec((nb, 3 * D_MODEL)),
        lspec((1, D_MODEL)),
        lspec((D_MODEL, N_PROJ * D_MODEL)),
        lspec((CONV_WIDTH, D_MODEL)),
        lspec((1, D_MODEL)),
        lspec((1, D_MODEL)),
        lspec((1, D_MODEL)),
        lspec((D_MODEL, D_MODEL)),
        lspec((len(POOL_WINDOWS), POOL_GROUP, POOL_GROUP)),
        lspec((1, D_MODEL)),
        lspec((D_MODEL, D_MODEL)),
        lspec((D_MODEL, D_MODEL)),
        cspec((1, D_MODEL)),
        cspec((1, D_MODEL)),
        lspec((nb, D_MODEL)),
        lspec((nb, D_MODEL)),
    ]
    out_specs = [cspec((nb, D_MODEL)), lspec((nb, D_MODEL)), lspec((nb, D_MODEL))]
    out_shape = [
        jax.ShapeDtypeStruct((nb, D_MODEL), F32),
        jax.ShapeDtypeStruct((DEPTH, nb, D_MODEL), F32),
        jax.ShapeDtypeStruct((DEPTH, nb, D_MODEL), F32),
    ]
    return pl.pallas_call(
        _sample_trunk_kernel,
        grid=(DEPTH,),
        in_specs=in_specs,
        out_specs=out_specs,
        out_shape=out_shape,
        scratch_shapes=[pltpu.VMEM((nb, D_MODEL), F32)],
        compiler_params=pltpu.CompilerParams(
            dimension_semantics=("arbitrary",), vmem_limit_bytes=VMEM_LIMIT_BYTES),
        name="sample_trunk",
    )(x, mod, p["g_norm"], p["w_in"], p["w_dw"], p["b_dw"], p["ln_g"], p["ln_b"], p["w_a_out"],
      p["w_grp"], p["pool_scale"], p["w_b_out"], p["w_out"], p["g_final"], pwinv, s_all, ps_all)


def _state_shift_kernel(s_ref, new_ref, o_ref, *, nbuf):
    for i in range(SAMPLE_ROWS):
        o_ref[i, 0:nbuf - 1, :] = s_ref[i, 1:nbuf, :]
        o_ref[i, nbuf - 1:nbuf, :] = new_ref[i:i + 1, :]


def _state_shift(state, new_rows, name):
    _, nb, nbuf, _ = state.shape
    rb = SAMPLE_ROWS
    return pl.pallas_call(
        functools.partial(_state_shift_kernel, nbuf=nbuf),
        grid=(DEPTH, nb // rb),
        in_specs=[
            pl.BlockSpec((None, rb, nbuf, D_MODEL), lambda l, i: (l, i, 0, 0)),
            pl.BlockSpec((None, rb, D_MODEL), lambda l, i: (l, i, 0)),
        ],
        out_specs=pl.BlockSpec((None, rb, nbuf, D_MODEL), lambda l, i: (l, i, 0, 0)),
        out_shape=jax.ShapeDtypeStruct(state.shape, F32),
        name=name,
    )(state, new_rows)


def kernel(x_prompt, x_sample, state_conv, state_pool, c_prompt, c_sample, w_ada, b_ada, g_norm, w_in, w_dw, b_dw, ln_g, ln_b, w_a_out, w_grp, pool_scale, w_b_out, w_out, g_final):
    n_sample = x_sample.shape[0]
    row = lambda a: a.reshape(DEPTH, 1, D_MODEL)
    p = dict(
        g_norm=row(g_norm), w_in=w_in.astype(BF16), w_dw=w_dw, b_dw=row(b_dw), ln_g=row(ln_g),
        ln_b=row(ln_b), w_a_out=w_a_out.astype(BF16), w_grp=w_grp.astype(BF16),
        pool_scale=row(pool_scale), w_b_out=w_b_out.astype(BF16), w_out=w_out.astype(BF16),
        g_final=g_final.reshape(1, D_MODEL),
    )
    win_lane = jnp.repeat(jnp.asarray(POOL_WINDOWS, F32), POOL_GROUP).reshape(1, D_MODEL)
    pmask = (jnp.arange(POOL_BUF, dtype=F32)[:, None] >= (POOL_BUF + 1) - win_lane).astype(F32)

    mod = _ada_mod(jnp.concatenate([c_sample, c_prompt], axis=0), w_ada, b_ada)

    x = x_prompt
    conv_p, pool_p = [], []
    for l in range(DEPTH):
        x, cn, pn = _prompt_layer(l, x, mod, n_sample, p, final=(l == DEPTH - 1))
        conv_p.append(cn)
        pool_p.append(pn)
    y_prompt = x

    s_all, ps_all = _state_reduce(state_conv, state_pool, w_dw, pmask)
    y_s, u_all, pu_all = _sample_trunk(x_sample[:, 0, :], mod, p, 1.0 / win_lane, s_all, ps_all)
    conv_s = _state_shift(state_conv, u_all, "sample_conv_state")
    pool_s = _state_shift(state_pool, pu_all, "sample_pool_state")

    return (y_prompt, y_s[:, None, :], jnp.stack(conv_p), conv_s, jnp.stack(pool_p), pool_s)
```

```python
import functools

import jax
import jax.numpy as jnp
from jax import lax
from jax.experimental import pallas as pl
from jax.experimental.pallas import tpu as pltpu

D_MODEL = 1024
DEPTH = 4
CONV_WIDTH = 31
CONV_BUF = CONV_WIDTH - 1
POOL_WINDOWS = (2, 4, 8, 16)
POOL_GROUP = D_MODEL // len(POOL_WINDOWS)
POOL_BUF = max(POOL_WINDOWS) - 1
N_PROJ = 7
RMS_EPS = 1e-6
LN_EPS = 1e-5

SUBLANES = 8
LANES = 128
HALO_C = 32
HALO_P = 16
VMEM_LIMIT_BYTES = 56 * 1024 * 1024

TILE_T = 256
CONV_GROUPS = 4
N_SLABS = D_MODEL // LANES
PHASES = 2
GROUP = SUBLANES * PHASES
SAMPLE_ROWS = 32

F32 = jnp.float32
BF16 = jnp.bfloat16


def _sigmoid(x):
    return jax.nn.sigmoid(x)


def _silu(x):
    return x * jax.nn.sigmoid(x)


def _rms(x, g):
    ms = jnp.mean(x * x, axis=-1, keepdims=True)
    return x * lax.rsqrt(ms + RMS_EPS) * g


def _layer_norm_silu(y, g, b):
    mu = jnp.mean(y, axis=-1, keepdims=True)
    yc = y - mu
    var = jnp.mean(yc * yc, axis=-1, keepdims=True)
    return _silu(yc * lax.rsqrt(var + LN_EPS) * g + b)


def _pack_rows(w):
    *lead, k, n = w.shape
    pairs = jnp.swapaxes(w.astype(BF16).reshape(*lead, k // 2, 2, n), -1, -2)
    return lax.bitcast_convert_type(pairs, jnp.uint32)


def _bf16_rows(w32):
    return pltpu.bitcast(w32, BF16)


def _lanes(j):
    return slice(j * LANES, (j + 1) * LANES)


def _col(blk):
    return slice(blk * D_MODEL, (blk + 1) * D_MODEL)


def _mod_kernel(c_ref, w_ref, b_ref, o_ref):
    c = c_ref[...]
    o_ref[...] = (
        jnp.dot(_silu(c).astype(BF16), w_ref[...].astype(BF16), preferred_element_type=F32)
        + b_ref[...]
    )


def _ada_mod(c_all, w_ada, b_ada):
    n = c_all.shape[0]
    return pl.pallas_call(
        _mod_kernel,
        grid=(DEPTH, 3),
        in_specs=[
            pl.BlockSpec((n, D_MODEL), lambda l, j: (0, 0)),
            pl.BlockSpec((None, D_MODEL, D_MODEL), lambda l, j: (l, 0, j)),
            pl.BlockSpec((None, 1, D_MODEL), lambda l, j: (l, 0, j)),
        ],
        out_specs=pl.BlockSpec((None, n, D_MODEL), lambda l, j: (l, 0, j)),
        out_shape=jax.ShapeDtypeStruct((DEPTH, n, 3 * D_MODEL), F32),
        name="ada_mod",
    )(c_all, w_ada, b_ada.reshape(DEPTH, 1, 3 * D_MODEL))


def _prompt_layer_kernel(
    xa_ref, xb_ref, mod_ref, gn_ref, win_ref, wdw_ref, bdw_ref, lng_ref, lnb_ref, waout_ref, wgrp_ref,
    psc_ref, wbout_ref, wout_ref, gfin_ref,
    xo_ref, cnew_ref, pnew_ref,
    *scratch,
    tt, n_tiles, tiles_per_seq, final,
):
    s = pl.program_id(0)

    for parity in range(2):
        pl.when(lax.rem(s, 2) == parity)(functools.partial(
            _prompt_step, parity, s,
            xa_ref, xb_ref, mod_ref, gn_ref, win_ref, wdw_ref, bdw_ref, lng_ref, lnb_ref, waout_ref,
            wgrp_ref, psc_ref, wbout_ref, wout_ref, gfin_ref, xo_ref, cnew_ref, pnew_ref,
            scratch[6 * parity:6 * parity + 6], scratch[6 * (1 - parity):6 * (1 - parity) + 6], *scratch[12:],
            tt=tt, n_tiles=n_tiles, tiles_per_seq=tiles_per_seq, final=final))


def _prompt_step(
    pa, s,
    xa_ref, xb_ref, mod_ref, gn_ref, win_ref, wdw_ref, bdw_ref, lng_ref, lnb_ref, waout_ref, wgrp_ref,
    psc_ref, wbout_ref, wout_ref, gfin_ref,
    xo_ref, cnew_ref, pnew_ref,
    fill, read, h_s, sg_s, y_s, q_s, a_s, p_s, b_s, m_s, ya_s, yb_s, o_s,
    *, tt, n_tiles, tiles_per_seq, final,
):
    extu_a, extp_a, sz_a, szp_a, sga_a, sgb_a = fill
    extu_b, extp_b, sz_b, szp_b, sga_b, sgb_b = read
    na = jnp.minimum(s, n_tiles - 1)
    nb = jnp.maximum(s - 1, 0)
    seq_a, first_a = lax.div(na, tiles_per_seq), lax.rem(na, tiles_per_seq) == 0
    seq_b, first_b = lax.div(nb, tiles_per_seq), lax.rem(nb, tiles_per_seq) == 0
    n_groups = tt // GROUP

    def rows(g):
        return slice(g * GROUP, (g + 1) * GROUP)

    def phase_rows(r0):
        return pl.ds(r0, SUBLANES, stride=PHASES)

    if pa == 0:
        @pl.when(s == 0)
        def _():
            for buf in read:
                buf[...] = jnp.zeros(buf.shape, F32)

    mod_a = mod_ref[pl.ds(seq_a, 1), :]
    shift = mod_a[:, 0:D_MODEL]
    gain = gn_ref[...] * (1.0 + mod_a[:, D_MODEL:2 * D_MODEL])
    gate = mod_ref[pl.ds(seq_b, 1), 2 * D_MODEL:3 * D_MODEL]

    def proj(blk):
        return jnp.dot(h_s[...], _bf16_rows(win_ref[:, _col(blk)]), preferred_element_type=F32)

    def a_norm():
        for g in range(n_groups):
            x = xa_ref[rows(g), :]
            ms = jnp.mean(x * x, axis=-1, keepdims=True)
            h_s[rows(g), :] = (x * lax.rsqrt(ms + RMS_EPS) * gain + shift).astype(BF16)

    def a_glu_gate():
        sg_s[...] = _sigmoid(proj(1))

    def a_glu():
        u = proj(0) * sg_s[...]
        for j in range(N_SLABS):
            extu_a[j, HALO_C:HALO_C + tt, :] = u[:, _lanes(j)]

    def a_pool_in():
        pu = proj(3)
        for j in range(N_SLABS):
            extp_a[j, HALO_P:HALO_P + tt, :] = pu[:, _lanes(j)]

    def a_history():
        for j in range(N_SLABS):
            extu_a[j, 0:HALO_C, :] = jnp.where(first_a, 0.0, extu_b[j, tt:tt + HALO_C, :])
            extp_a[j, 0:HALO_P, :] = jnp.where(first_a, 0.0, extp_b[j, tt:tt + HALO_P, :])
            cnew_ref[:, _lanes(j)] = extu_a[j, HALO_C + tt - CONV_BUF:HALO_C + tt, :]
            pnew_ref[:, _lanes(j)] = extp_a[j, HALO_P + tt - POOL_BUF:HALO_P + tt, :]

    def b_conv(j):
        bias = jnp.broadcast_to(bdw_ref[:, _lanes(j)], (SUBLANES, LANES))
        for g0 in range(0, n_groups, CONV_GROUPS):
            gs = range(g0, g0 + CONV_GROUPS)
            acc0 = {g: bias for g in gs}
            acc1 = {g: bias for g in gs}
            tap = prev_tap = None
            for k in range(CONV_WIDTH + 1):
                prev_tap = tap
                if k < CONV_WIDTH:
                    tap = jnp.broadcast_to(wdw_ref[k:k + 1, _lanes(j)], (SUBLANES, LANES))
                for g in gs:
                    d = extu_b[j, phase_rows(g * GROUP + HALO_C - CONV_BUF + k), :]
                    if k < CONV_WIDTH:
                        acc0[g] = acc0[g] + tap * d
                    if k >= 1:
                        acc1[g] = acc1[g] + prev_tap * d
            for g in gs:
                y_s[j, phase_rows(g * GROUP), :] = acc0[g]
                y_s[j, phase_rows(g * GROUP + 1), :] = acc1[g]

    def b_norm():
        for g in range(n_groups):
            y = jnp.concatenate([y_s[j, rows(g), :] for j in range(N_SLABS)], axis=-1)
            a = _layer_norm_silu(y, lng_ref[...], lnb_ref[...]) * sz_b[rows(g), :]
            a_s[rows(g), :] = a.astype(BF16)

    def b_pool():
        pos1 = [(lax.broadcasted_iota(jnp.int32, (SUBLANES, LANES), 0) * PHASES + (ph + 1)).astype(F32)
                for ph in range(PHASES)]
        for j in range(N_SLABS):
            w = POOL_WINDOWS[j * LANES // POOL_GROUP]
            inv_first = [jnp.where(first_b, 1.0 / jnp.minimum(pos1[ph], float(w)), 1.0 / w)
                         for ph in range(PHASES)]
            for g in range(n_groups):
                base = g * GROUP + HALO_P
                d = {i: extp_b[j, phase_rows(base + i), :] for i in range(1 - w, 2)}
                common = d[0]
                for i in range(1, w - 1):
                    common = common + d[-i]
                s0 = common + d[1 - w]
                s1 = common + d[1]
                if g == 0:
                    q0, q1 = s0 * inv_first[0] - d[0], s1 * inv_first[1] - d[1]
                else:
                    q0, q1 = s0 * (1.0 / w) - d[0], s1 * (1.0 / w) - d[1]
                q_s[j, phase_rows(g * GROUP), :] = q0
                q_s[j, phase_rows(g * GROUP + 1), :] = q1
        for g in range(n_groups):
            p = jnp.concatenate([q_s[j, rows(g), :] for j in range(N_SLABS)], axis=-1)
            p_s[rows(g), :] = p.astype(BF16)

    def b_group_dots():
        for gi in range(len(POOL_WINDOWS)):
            gl = slice(gi * POOL_GROUP, (gi + 1) * POOL_GROUP)
            yb_s[:, gl] = jnp.dot(p_s[:, gl], _bf16_rows(wgrp_ref[gi]), preferred_element_type=F32)

    def b_pool_gate():
        for g in range(n_groups):
            b_s[rows(g), :] = (yb_s[rows(g), :] * psc_ref[...] * szp_b[rows(g), :]).astype(BF16)

    def b_merge():
        for g in range(n_groups):
            m = sga_b[rows(g), :] * ya_s[rows(g), :] + sgb_b[rows(g), :] * o_s[rows(g), :]
            m_s[rows(g), :] = m.astype(BF16)

    def b_residual():
        for g in range(n_groups):
            xn = xb_ref[rows(g), :] + gate * yb_s[rows(g), :]
            if final:
                xn = _rms(xn, gfin_ref[...])
            xo_ref[rows(g), :] = xn

    a_norm()
    a_glu_gate()
    b_conv(0), b_conv(1)
    a_glu()
    b_conv(2), b_conv(3)
    a_pool_in()
    b_conv(4), b_conv(5)
    sz_a[...] = _silu(proj(2))
    b_conv(6), b_conv(7)
    a_history()
    szp_a[...] = _silu(proj(4))
    b_norm()
    b_pool()
    ya_s[...] = jnp.dot(a_s[...], _bf16_rows(waout_ref[...]), preferred_element_type=F32)
    b_group_dots()
    sga_a[...] = _sigmoid(proj(5))
    b_pool_gate()
    o_s[...] = jnp.dot(b_s[...], _bf16_rows(wbout_ref[...]), preferred_element_type=F32)
    sgb_a[...] = _sigmoid(proj(6))
    b_merge()
    yb_s[...] = jnp.dot(m_s[...], _bf16_rows(wout_ref[...]), preferred_element_type=F32)
    b_residual()


def _prompt_layer(l, x, mod, n_sample, p, final):
    bsz, seq, _ = x.shape
    tt = TILE_T
    tiles_per_seq = seq // tt
    n_tiles = bsz * tiles_per_seq
    kern = functools.partial(
        _prompt_layer_kernel, tt=tt, n_tiles=n_tiles, tiles_per_seq=tiles_per_seq, final=final)
    mod_blk = n_sample // bsz

    def tile_a(s):
        n = jnp.minimum(s, n_tiles - 1)
        return (n // tiles_per_seq, n % tiles_per_seq, 0)

    def tile_b(s):
        n = jnp.maximum(s - 1, 0)
        return (n // tiles_per_seq, n % tiles_per_seq, 0)

    def resident(shape, index):
        return pl.BlockSpec(shape, lambda s: index, pipeline_mode=pl.Buffered(1))

    def layer(shape):
        return resident((None,) + shape, (l,) + (0,) * len(shape))

    in_specs = [
        pl.BlockSpec((None, tt, D_MODEL), tile_a),
        pl.BlockSpec((None, tt, D_MODEL), tile_b),
        resident((None, bsz, 3 * D_MODEL), (l, mod_blk, 0)),
        layer((1, D_MODEL)),
        layer((D_MODEL // 2, N_PROJ * D_MODEL)),
        layer((CONV_WIDTH, D_MODEL)),
        layer((1, D_MODEL)),
        layer((1, D_MODEL)),
        layer((1, D_MODEL)),
        layer((D_MODEL // 2, D_MODEL)),
        layer((len(POOL_WINDOWS), POOL_GROUP // 2, POOL_GROUP)),
        layer((1, D_MODEL)),
        layer((D_MODEL // 2, D_MODEL)),
        layer((D_MODEL // 2, D_MODEL)),
        resident((1, D_MODEL), (0, 0)),
    ]
    out_specs = [
        pl.BlockSpec((None, tt, D_MODEL), tile_b),
        pl.BlockSpec((None, CONV_BUF, D_MODEL), lambda s: (jnp.minimum(s, n_tiles - 1) // tiles_per_seq, 0, 0)),
        pl.BlockSpec((None, POOL_BUF, D_MODEL), lambda s: (jnp.minimum(s, n_tiles - 1) // tiles_per_seq, 0, 0)),
    ]
    out_shape = [
        jax.ShapeDtypeStruct((bsz, seq, D_MODEL), F32),
        jax.ShapeDtypeStruct((bsz, CONV_BUF, D_MODEL), F32),
        jax.ShapeDtypeStruct((bsz, POOL_BUF, D_MODEL), F32),
    ]
    act_bf16 = pltpu.VMEM((tt, D_MODEL), BF16)
    act_f32 = pltpu.VMEM((tt, D_MODEL), F32)
    slabs = pltpu.VMEM((N_SLABS, tt, LANES), F32)
    handoff = [
        pltpu.VMEM((N_SLABS, HALO_C + tt, LANES), F32),
        pltpu.VMEM((N_SLABS, HALO_P + tt, LANES), F32),
        act_f32, act_f32, act_f32, act_f32,
    ]
    scratch = handoff + handoff + [
        act_bf16,
        act_f32,
        slabs, slabs,
        act_bf16, act_bf16, act_bf16, act_bf16,
        act_f32, act_f32, act_f32,
    ]
    return pl.pallas_call(
        kern,
        grid=(n_tiles + 1,),
        in_specs=in_specs,
        out_specs=out_specs,
        out_shape=out_shape,
        scratch_shapes=scratch,
        compiler_params=pltpu.CompilerParams(
            dimension_semantics=("arbitrary",), vmem_limit_bytes=VMEM_LIMIT_BYTES),
        name=f"prompt_layer{l}",
    )(x, x, mod, p["g_norm"], p["w_in"], p["w_dw"], p["b_dw"], p["ln_g"], p["ln_b"], p["w_a_out"],
      p["w_grp"], p["pool_scale"], p["w_b_out"], p["w_out"], p["g_final"])


def _state_reduce_kernel(sc_ref, sp_ref, wdw_ref, s_ref, ps_ref):
    acc = wdw_ref[0:1, :] * sc_ref[0]
    for k in range(1, CONV_BUF):
        acc = acc + wdw_ref[k:k + 1, :] * sc_ref[k]
    s_ref[...] = acc
    parts = []
    for gi, w in enumerate(POOL_WINDOWS):
        gl = slice(gi * POOL_GROUP, (gi + 1) * POOL_GROUP)
        ps = sp_ref[POOL_BUF - 1, :, gl]
        for k in range(POOL_BUF - 2, POOL_BUF - w, -1):
            ps = ps + sp_ref[k, :, gl]
        parts.append(ps)
    ps_ref[...] = jnp.concatenate(parts, axis=-1)


def _state_reduce(sc_t, sp_t, w_dw):
    nb = sc_t.shape[2]
    rb = SAMPLE_ROWS
    return pl.pallas_call(
        _state_reduce_kernel,
        grid=(DEPTH, nb // rb),
        in_specs=[
            pl.BlockSpec((None, CONV_BUF, rb, D_MODEL), lambda l, i: (l, 0, i, 0)),
            pl.BlockSpec((None, POOL_BUF, rb, D_MODEL), lambda l, i: (l, 0, i, 0)),
            pl.BlockSpec((None, CONV_WIDTH, D_MODEL), lambda l, i: (l, 0, 0)),
        ],
        out_specs=[
            pl.BlockSpec((None, rb, D_MODEL), lambda l, i: (l, i, 0)),
            pl.BlockSpec((None, rb, D_MODEL), lambda l, i: (l, i, 0)),
        ],
        out_shape=[jax.ShapeDtypeStruct((DEPTH, nb, D_MODEL), F32)] * 2,
        name="sample_state_reduce",
    )(sc_t, sp_t, w_dw)


def _sample_trunk_kernel(
    x_ref, mod_ref, gn_ref, win_ref, wdw_ref, bdw_ref, lng_ref, lnb_ref, waout_ref, wgrp_ref,
    psc_ref, wbout_ref, wout_ref, gfin_ref, pwinv_ref, s_ref, ps_ref,
    y_ref, u_ref, pu_ref,
    xs,
):
    l = pl.program_id(0)

    @pl.when(l == 0)
    def _():
        xs[...] = x_ref[...]

    x = xs[...]
    mod = mod_ref[...]
    shift = mod[:, 0:D_MODEL]
    scale = mod[:, D_MODEL:2 * D_MODEL]
    gate = mod[:, 2 * D_MODEL:3 * D_MODEL]
    h = (_rms(x, gn_ref[...]) * (1.0 + scale) + shift).astype(BF16)
    proj = jnp.dot(h, _bf16_rows(win_ref[...]), preferred_element_type=F32)

    def col(j):
        return proj[:, _col(j)]

    u = col(0) * _sigmoid(col(1))
    u_ref[...] = u
    y = s_ref[...] + wdw_ref[CONV_BUF:CONV_WIDTH, :] * u + bdw_ref[...]
    a = _layer_norm_silu(y, lng_ref[...], lnb_ref[...]) * _silu(col(2))
    y_a = jnp.dot(a.astype(BF16), _bf16_rows(waout_ref[...]), preferred_element_type=F32)

    pu = col(3)
    pu_ref[...] = pu
    pm = ((ps_ref[...] + pu) * pwinv_ref[...] - pu).astype(BF16)
    q = jnp.concatenate(
        [jnp.dot(pm[:, g * POOL_GROUP:(g + 1) * POOL_GROUP], _bf16_rows(wgrp_ref[g]), preferred_element_type=F32)
         for g in range(len(POOL_WINDOWS))], axis=-1)
    bq = q * psc_ref[...] * _silu(col(4))
    y_b = jnp.dot(bq.astype(BF16), _bf16_rows(wbout_ref[...]), preferred_element_type=F32)

    m = _sigmoid(col(5)) * y_a + _sigmoid(col(6)) * y_b
    xn = x + gate * jnp.dot(m.astype(BF16), _bf16_rows(wout_ref[...]), preferred_element_type=F32)
    xs[...] = xn

    @pl.when(l == DEPTH - 1)
    def _():
        y_ref[...] = _rms(xn, gfin_ref[...])


def _sample_trunk(x, mod, p, pwinv, s_all, ps_all):
    nb = x.shape[0]

    def lspec(shape):
        nd = len(shape)
        return pl.BlockSpec((None,) + shape, lambda l: (l,) + (0,) * nd)

    def cspec(shape):
        nd = len(shape)
        return pl.BlockSpec(shape, lambda l: (0,) * nd)

    in_specs = [
        cspec((nb, D_MODEL)),
        lspec((nb, 3 * D_MODEL)),
        lspec((1, D_MODEL)),
        lspec((D_MODEL // 2, N_PROJ * D_MODEL)),
        lspec((CONV_WIDTH, D_MODEL)),
        lspec((1, D_MODEL)),
        lspec((1, D_MODEL)),
        lspec((1, D_MODEL)),
        lspec((D_MODEL // 2, D_MODEL)),
        lspec((len(POOL_WINDOWS), POOL_GROUP // 2, POOL_GROUP)),
        lspec((1, D_MODEL)),
        lspec((D_MODEL // 2, D_MODEL)),
        lspec((D_MODEL // 2, D_MODEL)),
        cspec((1, D_MODEL)),
        cspec((1, D_MODEL)),
        lspec((nb, D_MODEL)),
        lspec((nb, D_MODEL)),
    ]
    out_specs = [cspec((nb, D_MODEL)), lspec((nb, D_MODEL)), lspec((nb, D_MODEL))]
    out_shape = [
        jax.ShapeDtypeStruct((nb, D_MODEL), F32),
        jax.ShapeDtypeStruct((DEPTH, nb, D_MODEL), F32),
        jax.ShapeDtypeStruct((DEPTH, nb, D_MODEL), F32),
    ]
    return pl.pallas_call(
        _sample_trunk_kernel,
        grid=(DEPTH,),
        in_specs=in_specs,
        out_specs=out_specs,
        out_shape=out_shape,
        scratch_shapes=[pltpu.VMEM((nb, D_MODEL), F32)],
        compiler_params=pltpu.CompilerParams(
            dimension_semantics=("arbitrary",), vmem_limit_bytes=VMEM_LIMIT_BYTES),
        name="sample_trunk",
    )(x, mod, p["g_norm"], p["w_in"], p["w_dw"], p["b_dw"], p["ln_g"], p["ln_b"], p["w_a_out"],
      p["w_grp"], p["pool_scale"], p["w_b_out"], p["w_out"], p["g_final"], pwinv, s_all, ps_all)


def _state_shift_copies(sc_hbm, sp_hbm, u_hbm, pu_hbm, oc_hbm, op_hbm, sems):
    copies = []
    for l in range(DEPTH):
        for old, new, out, nbuf in ((sc_hbm, u_hbm, oc_hbm, CONV_BUF), (sp_hbm, pu_hbm, op_hbm, POOL_BUF)):
            copies.append((old.at[l, pl.ds(1, nbuf - 1)], out.at[l, pl.ds(0, nbuf - 1)]))
            copies.append((new.at[l], out.at[l, nbuf - 1]))
    return [pltpu.make_async_copy(src, dst, sems.at[i]) for i, (src, dst) in enumerate(copies)]


def _state_shift_kernel(sc_hbm, sp_hbm, u_hbm, pu_hbm, oc_hbm, op_hbm, sems):
    copies = _state_shift_copies(sc_hbm, sp_hbm, u_hbm, pu_hbm, oc_hbm, op_hbm, sems)
    for c in copies:
        c.start()
    for c in copies:
        c.wait()


def _state_shift(sc_t, sp_t, u_all, pu_all):
    any_spec = pl.BlockSpec(memory_space=pl.ANY)
    return pl.pallas_call(
        _state_shift_kernel,
        in_specs=[any_spec] * 4,
        out_specs=[any_spec] * 2,
        out_shape=[jax.ShapeDtypeStruct(sc_t.shape, F32), jax.ShapeDtypeStruct(sp_t.shape, F32)],
        scratch_shapes=[pltpu.SemaphoreType.DMA((4 * DEPTH,))],
        name="sample_state_shift",
    )(sc_t, sp_t, u_all, pu_all)


def kernel(x_prompt, x_sample, state_conv, state_pool, c_prompt, c_sample, w_ada, b_ada, g_norm, w_in, w_dw, b_dw, ln_g, ln_b, w_a_out, w_grp, pool_scale, w_b_out, w_out, g_final):
    n_sample = x_sample.shape[0]
    row = lambda a: a.reshape(DEPTH, 1, D_MODEL)
    p = dict(
        g_norm=row(g_norm), w_in=_pack_rows(w_in), w_dw=w_dw, b_dw=row(b_dw), ln_g=row(ln_g),
        ln_b=row(ln_b), w_a_out=_pack_rows(w_a_out), w_grp=_pack_rows(w_grp),
        pool_scale=row(pool_scale), w_b_out=_pack_rows(w_b_out), w_out=_pack_rows(w_out),
        g_final=g_final.reshape(1, D_MODEL),
    )
    win_lane = jnp.repeat(jnp.asarray(POOL_WINDOWS, F32), POOL_GROUP).reshape(1, D_MODEL)

    mod = _ada_mod(jnp.concatenate([c_sample, c_prompt], axis=0), w_ada, b_ada)

    x = x_prompt
    conv_p, pool_p = [], []
    for l in range(DEPTH):
        x, cn, pn = _prompt_layer(l, x, mod, n_sample, p, final=(l == DEPTH - 1))
        conv_p.append(cn)
        pool_p.append(pn)
    y_prompt = x

    sc_t = jnp.transpose(state_conv, (0, 2, 1, 3))
    sp_t = jnp.transpose(state_pool, (0, 2, 1, 3))
    s_all, ps_all = _state_reduce(sc_t, sp_t, w_dw)
    y_s, u_all, pu_all = _sample_trunk(x_sample[:, 0, :], mod, p, 1.0 / win_lane, s_all, ps_all)
    conv_t, pool_t = _state_shift(sc_t, sp_t, u_all, pu_all)
    conv_s = jnp.transpose(conv_t, (0, 2, 1, 3))
    pool_s = jnp.transpose(pool_t, (0, 2, 1, 3))

    return (y_prompt, y_s[:, None, :], jnp.stack(conv_p), conv_s, jnp.stack(pool_p), pool_s)
```

```python
import functools

import jax
import jax.numpy as jnp
from jax import lax
from jax.experimental import pallas as pl
from jax.experimental.pallas import tpu as pltpu

D_MODEL = 1024
DEPTH = 4
CONV_WIDTH = 31
CONV_BUF = CONV_WIDTH - 1
POOL_WINDOWS = (2, 4, 8, 16)
POOL_GROUP = D_MODEL // len(POOL_WINDOWS)
POOL_BUF = max(POOL_WINDOWS) - 1
N_PROJ = 7
RMS_EPS = 1e-6
LN_EPS = 1e-5

SUBLANES = 8
LANES = 128
HALO_C = 32
HALO_P = 16
VMEM_LIMIT_BYTES = 56 * 1024 * 1024

TILE_T = 256
CONV_GROUPS = 4
N_SLABS = D_MODEL // LANES
PHASES = 2
GROUP = SUBLANES * PHASES
SAMPLE_ROWS = 32
SHIFT_ROWS = 5
PACK_BLOCK = (512, 1024)

F32 = jnp.float32
BF16 = jnp.bfloat16


def _sigmoid(x):
    return jax.nn.sigmoid(x)


def _silu(x):
    return x * jax.nn.sigmoid(x)


def _rms(x, g):
    ms = jnp.mean(x * x, axis=-1, keepdims=True)
    return x * lax.rsqrt(ms + RMS_EPS) * g


def _layer_norm_silu(y, g, b):
    mu = jnp.mean(y, axis=-1, keepdims=True)
    yc = y - mu
    var = jnp.mean(yc * yc, axis=-1, keepdims=True)
    return _silu(yc * lax.rsqrt(var + LN_EPS) * g + b)


def _pack_kernel(w_ref, o_ref):
    o_ref[...] = pltpu.bitcast(w_ref[...].astype(BF16), jnp.uint32)


def _pack_rows(w, name):
    g, k, n = w.shape
    bk, bn = min(k, PACK_BLOCK[0]), min(n, PACK_BLOCK[1])
    return pl.pallas_call(
        _pack_kernel,
        grid=(g, k // bk, n // bn),
        in_specs=[pl.BlockSpec((None, bk, bn), lambda i, a, b: (i, a, b))],
        out_specs=pl.BlockSpec((None, bk // 2, bn), lambda i, a, b: (i, a, b)),
        out_shape=jax.ShapeDtypeStruct((g, k // 2, n), jnp.uint32),
        name=name,
    )(w)


def _bf16_rows(w32):
    return pltpu.bitcast(w32, BF16)


def _lanes(j):
    return slice(j * LANES, (j + 1) * LANES)


def _col(blk):
    return slice(blk * D_MODEL, (blk + 1) * D_MODEL)


def _mod_kernel(c_ref, w_ref, b_ref, o_ref):
    c = c_ref[...]
    o_ref[...] = (
        jnp.dot(_silu(c).astype(BF16), w_ref[...].astype(BF16), preferred_element_type=F32)
        + b_ref[...]
    )


def _ada_mod(c_all, w_ada, b_ada):
    n = c_all.shape[0]
    return pl.pallas_call(
        _mod_kernel,
        grid=(DEPTH, 3),
        in_specs=[
            pl.BlockSpec((n, D_MODEL), lambda l, j: (0, 0)),
            pl.BlockSpec((None, D_MODEL, D_MODEL), lambda l, j: (l, 0, j)),
            pl.BlockSpec((None, 1, D_MODEL), lambda l, j: (l, 0, j)),
        ],
        out_specs=pl.BlockSpec((None, n, D_MODEL), lambda l, j: (l, 0, j)),
        out_shape=jax.ShapeDtypeStruct((DEPTH, n, 3 * D_MODEL), F32),
        name="ada_mod",
    )(c_all, w_ada, b_ada.reshape(DEPTH, 1, 3 * D_MODEL))


def _prompt_layer_kernel(
    xa_ref, xb_ref, mod_ref, gn_ref, win_ref, wdw_ref, bdw_ref, lng_ref, lnb_ref, waout_ref, wgrp_ref,
    psc_ref, wbout_ref, wout_ref, gfin_ref,
    xo_ref, cnew_ref, pnew_ref,
    *scratch,
    tt, n_tiles, tiles_per_seq, final,
):
    s = pl.program_id(0)

    for parity in range(2):
        pl.when(lax.rem(s, 2) == parity)(functools.partial(
            _prompt_step, parity, s,
            xa_ref, xb_ref, mod_ref, gn_ref, win_ref, wdw_ref, bdw_ref, lng_ref, lnb_ref, waout_ref,
            wgrp_ref, psc_ref, wbout_ref, wout_ref, gfin_ref, xo_ref, cnew_ref, pnew_ref,
            scratch[6 * parity:6 * parity + 6], scratch[6 * (1 - parity):6 * (1 - parity) + 6], *scratch[12:],
            tt=tt, n_tiles=n_tiles, tiles_per_seq=tiles_per_seq, final=final))


def _prompt_step(
    pa, s,
    xa_ref, xb_ref, mod_ref, gn_ref, win_ref, wdw_ref, bdw_ref, lng_ref, lnb_ref, waout_ref, wgrp_ref,
    psc_ref, wbout_ref, wout_ref, gfin_ref,
    xo_ref, cnew_ref, pnew_ref,
    fill, read, h_s, sg_s, y_s, q_s, a_s, p_s, b_s, m_s, ya_s, yb_s, o_s,
    *, tt, n_tiles, tiles_per_seq, final,
):
    extu_a, extp_a, sz_a, szp_a, sga_a, sgb_a = fill
    extu_b, extp_b, sz_b, szp_b, sga_b, sgb_b = read
    na = jnp.minimum(s, n_tiles - 1)
    nb = jnp.maximum(s - 1, 0)
    seq_a, first_a = lax.div(na, tiles_per_seq), lax.rem(na, tiles_per_seq) == 0
    seq_b, first_b = lax.div(nb, tiles_per_seq), lax.rem(nb, tiles_per_seq) == 0
    n_groups = tt // GROUP

    def rows(g):
        return slice(g * GROUP, (g + 1) * GROUP)

    def phase_rows(r0):
        return pl.ds(r0, SUBLANES, stride=PHASES)

    if pa == 0:
        @pl.when(s == 0)
        def _():
            for buf in read:
                buf[...] = jnp.zeros(buf.shape, F32)

    mod_a = mod_ref[pl.ds(seq_a, 1), :]
    shift = mod_a[:, 0:D_MODEL]
    gain = gn_ref[...] * (1.0 + mod_a[:, D_MODEL:2 * D_MODEL])
    gate = mod_ref[pl.ds(seq_b, 1), 2 * D_MODEL:3 * D_MODEL]

    def proj(blk):
        return jnp.dot(h_s[...], _bf16_rows(win_ref[:, _col(blk)]), preferred_element_type=F32)

    def a_norm():
        for g in range(n_groups):
            x = xa_ref[rows(g), :]
            ms = jnp.mean(x * x, axis=-1, keepdims=True)
            h_s[rows(g), :] = (x * lax.rsqrt(ms + RMS_EPS) * gain + shift).astype(BF16)

    def a_glu_gate():
        sg_s[...] = _sigmoid(proj(1))

    def a_glu():
        u = proj(0) * sg_s[...]
        for j in range(N_SLABS):
            extu_a[j, HALO_C:HALO_C + tt, :] = u[:, _lanes(j)]

    def a_pool_in():
        pu = proj(3)
        for j in range(N_SLABS):
            extp_a[j, HALO_P:HALO_P + tt, :] = pu[:, _lanes(j)]

    def a_history():
        for j in range(N_SLABS):
            extu_a[j, 0:HALO_C, :] = jnp.where(first_a, 0.0, extu_b[j, tt:tt + HALO_C, :])
            extp_a[j, 0:HALO_P, :] = jnp.where(first_a, 0.0, extp_b[j, tt:tt + HALO_P, :])
            cnew_ref[:, _lanes(j)] = extu_a[j, HALO_C + tt - CONV_BUF:HALO_C + tt, :]
            pnew_ref[:, _lanes(j)] = extp_a[j, HALO_P + tt - POOL_BUF:HALO_P + tt, :]

    def b_conv(j):
        bias = jnp.broadcast_to(bdw_ref[:, _lanes(j)], (SUBLANES, LANES))
        for g0 in range(0, n_groups, CONV_GROUPS):
            gs = range(g0, g0 + CONV_GROUPS)
            acc0 = {g: bias for g in gs}
            acc1 = {g: bias for g in gs}
            tap = prev_tap = None
            for k in range(CONV_WIDTH + 1):
                prev_tap = tap
                if k < CONV_WIDTH:
                    tap = jnp.broadcast_to(wdw_ref[k:k + 1, _lanes(j)], (SUBLANES, LANES))
                for g in gs:
                    d = extu_b[j, phase_rows(g * GROUP + HALO_C - CONV_BUF + k), :]
                    if k < CONV_WIDTH:
                        acc0[g] = acc0[g] + tap * d
                    if k >= 1:
                        acc1[g] = acc1[g] + prev_tap * d
            for g in gs:
                y_s[j, phase_rows(g * GROUP), :] = acc0[g]
                y_s[j, phase_rows(g * GROUP + 1), :] = acc1[g]

    def b_norm():
        for g in range(n_groups):
            y = jnp.concatenate([y_s[j, rows(g), :] for j in range(N_SLABS)], axis=-1)
            a = _layer_norm_silu(y, lng_ref[...], lnb_ref[...]) * sz_b[rows(g), :]
            a_s[rows(g), :] = a.astype(BF16)

    def b_pool():
        pos1 = [(lax.broadcasted_iota(jnp.int32, (SUBLANES, LANES), 0) * PHASES + (ph + 1)).astype(F32)
                for ph in range(PHASES)]
        for j in range(N_SLABS):
            w = POOL_WINDOWS[j * LANES // POOL_GROUP]
            inv_first = [jnp.where(first_b, 1.0 / jnp.minimum(pos1[ph], float(w)), 1.0 / w)
                         for ph in range(PHASES)]
            for g in range(n_groups):
                base = g * GROUP + HALO_P
                d = {i: extp_b[j, phase_rows(base + i), :] for i in range(1 - w, 2)}
                common = d[0]
                for i in range(1, w - 1):
                    common = common + d[-i]
                s0 = common + d[1 - w]
                s1 = common + d[1]
                if g == 0:
                    q0, q1 = s0 * inv_first[0] - d[0], s1 * inv_first[1] - d[1]
                else:
                    q0, q1 = s0 * (1.0 / w) - d[0], s1 * (1.0 / w) - d[1]
                q_s[j, phase_rows(g * GROUP), :] = q0
                q_s[j, phase_rows(g * GROUP + 1), :] = q1
        for g in range(n_groups):
            p = jnp.concatenate([q_s[j, rows(g), :] for j in range(N_SLABS)], axis=-1)
            p_s[rows(g), :] = p.astype(BF16)

    def b_group_dots():
        for gi in range(len(POOL_WINDOWS)):
            gl = slice(gi * POOL_GROUP, (gi + 1) * POOL_GROUP)
            yb_s[:, gl] = jnp.dot(p_s[:, gl], _bf16_rows(wgrp_ref[gi]), preferred_element_type=F32)

    def b_pool_gate():
        for g in range(n_groups):
            b_s[rows(g), :] = (yb_s[rows(g), :] * psc_ref[...] * szp_b[rows(g), :]).astype(BF16)

    def b_merge():
        for g in range(n_groups):
            m = sga_b[rows(g), :] * ya_s[rows(g), :] + sgb_b[rows(g), :] * o_s[rows(g), :]
            m_s[rows(g), :] = m.astype(BF16)

    def b_residual():
        for g in range(n_groups):
            xn = xb_ref[rows(g), :] + gate * yb_s[rows(g), :]
            if final:
                xn = _rms(xn, gfin_ref[...])
            xo_ref[rows(g), :] = xn

    a_norm()
    a_glu_gate()
    b_conv(0), b_conv(1)
    a_glu()
    b_conv(2), b_conv(3)
    a_pool_in()
    b_conv(4), b_conv(5)
    sz_a[...] = _silu(proj(2))
    b_conv(6), b_conv(7)
    a_history()
    szp_a[...] = _silu(proj(4))
    b_norm()
    b_pool()
    ya_s[...] = jnp.dot(a_s[...], _bf16_rows(waout_ref[...]), preferred_element_type=F32)
    b_group_dots()
    sga_a[...] = _sigmoid(proj(5))
    b_pool_gate()
    o_s[...] = jnp.dot(b_s[...], _bf16_rows(wbout_ref[...]), preferred_element_type=F32)
    sgb_a[...] = _sigmoid(proj(6))
    b_merge()
    yb_s[...] = jnp.dot(m_s[...], _bf16_rows(wout_ref[...]), preferred_element_type=F32)
    b_residual()


def _prompt_layer(l, x, mod, n_sample, p, final):
    bsz, seq, _ = x.shape
    tt = TILE_T
    tiles_per_seq = seq // tt
    n_tiles = bsz * tiles_per_seq
    kern = functools.partial(
        _prompt_layer_kernel, tt=tt, n_tiles=n_tiles, tiles_per_seq=tiles_per_seq, final=final)
    mod_blk = n_sample // bsz

    def tile_a(s):
        n = jnp.minimum(s, n_tiles - 1)
        return (n // tiles_per_seq, n % tiles_per_seq, 0)

    def tile_b(s):
        n = jnp.maximum(s - 1, 0)
        return (n // tiles_per_seq, n % tiles_per_seq, 0)

    def resident(shape, index):
        return pl.BlockSpec(shape, lambda s: index, pipeline_mode=pl.Buffered(1))

    def layer(shape):
        return resident((None,) + shape, (l,) + (0,) * len(shape))

    in_specs = [
        pl.BlockSpec((None, tt, D_MODEL), tile_a),
        pl.BlockSpec((None, tt, D_MODEL), tile_b),
        resident((None, bsz, 3 * D_MODEL), (l, mod_blk, 0)),
        layer((1, D_MODEL)),
        layer((D_MODEL // 2, N_PROJ * D_MODEL)),
        layer((CONV_WIDTH, D_MODEL)),
        layer((1, D_MODEL)),
        layer((1, D_MODEL)),
        layer((1, D_MODEL)),
        layer((D_MODEL // 2, D_MODEL)),
        layer((len(POOL_WINDOWS), POOL_GROUP // 2, POOL_GROUP)),
        layer((1, D_MODEL)),
        layer((D_MODEL // 2, D_MODEL)),
        layer((D_MODEL // 2, D_MODEL)),
        resident((1, D_MODEL), (0, 0)),
    ]
    out_specs = [
        pl.BlockSpec((None, tt, D_MODEL), tile_b),
        pl.BlockSpec((None, CONV_BUF, D_MODEL), lambda s: (jnp.minimum(s, n_tiles - 1) // tiles_per_seq, 0, 0)),
        pl.BlockSpec((None, POOL_BUF, D_MODEL), lambda s: (jnp.minimum(s, n_tiles - 1) // tiles_per_seq, 0, 0)),
    ]
    out_shape = [
        jax.ShapeDtypeStruct((bsz, seq, D_MODEL), F32),
        jax.ShapeDtypeStruct((bsz, CONV_BUF, D_MODEL), F32),
        jax.ShapeDtypeStruct((bsz, POOL_BUF, D_MODEL), F32),
    ]
    act_bf16 = pltpu.VMEM((tt, D_MODEL), BF16)
    act_f32 = pltpu.VMEM((tt, D_MODEL), F32)
    slabs = pltpu.VMEM((N_SLABS, tt, LANES), F32)
    handoff = [
        pltpu.VMEM((N_SLABS, HALO_C + tt, LANES), F32),
        pltpu.VMEM((N_SLABS, HALO_P + tt, LANES), F32),
        act_f32, act_f32, act_f32, act_f32,
    ]
    scratch = handoff + handoff + [
        act_bf16,
        act_f32,
        slabs, slabs,
        act_bf16, act_bf16, act_bf16, act_bf16,
        act_f32, act_f32, act_f32,
    ]
    return pl.pallas_call(
        kern,
        grid=(n_tiles + 1,),
        in_specs=in_specs,
        out_specs=out_specs,
        out_shape=out_shape,
        scratch_shapes=scratch,
        compiler_params=pltpu.CompilerParams(
            dimension_semantics=("arbitrary",), vmem_limit_bytes=VMEM_LIMIT_BYTES),
        name=f"prompt_layer{l}",
    )(x, x, mod, p["g_norm"], p["w_in"], p["w_dw"], p["b_dw"], p["ln_g"], p["ln_b"], p["w_a_out"],
      p["w_grp"], p["pool_scale"], p["w_b_out"], p["w_out"], p["g_final"])


def _state_reduce_kernel(sc_ref, sp_ref, wdw_ref, s_ref, ps_ref):
    acc = wdw_ref[0:1, :] * sc_ref[0]
    for k in range(1, CONV_BUF):
        acc = acc + wdw_ref[k:k + 1, :] * sc_ref[k]
    s_ref[...] = acc
    parts = []
    for gi, w in enumerate(POOL_WINDOWS):
        gl = slice(gi * POOL_GROUP, (gi + 1) * POOL_GROUP)
        ps = sp_ref[POOL_BUF - 1, :, gl]
        for k in range(POOL_BUF - 2, POOL_BUF - w, -1):
            ps = ps + sp_ref[k, :, gl]
        parts.append(ps)
    ps_ref[...] = jnp.concatenate(parts, axis=-1)


def _state_reduce(sc_t, sp_t, w_dw):
    nb = sc_t.shape[2]
    rb = SAMPLE_ROWS
    return pl.pallas_call(
        _state_reduce_kernel,
        grid=(DEPTH, nb // rb),
        in_specs=[
            pl.BlockSpec((None, CONV_BUF, rb, D_MODEL), lambda l, i: (l, 0, i, 0)),
            pl.BlockSpec((None, POOL_BUF, rb, D_MODEL), lambda l, i: (l, 0, i, 0)),
            pl.BlockSpec((None, CONV_WIDTH, D_MODEL), lambda l, i: (l, 0, 0)),
        ],
        out_specs=[
            pl.BlockSpec((None, rb, D_MODEL), lambda l, i: (l, i, 0)),
            pl.BlockSpec((None, rb, D_MODEL), lambda l, i: (l, i, 0)),
        ],
        out_shape=[jax.ShapeDtypeStruct((DEPTH, nb, D_MODEL), F32)] * 2,
        name="sample_state_reduce",
    )(sc_t, sp_t, w_dw)


def _sample_trunk_kernel(
    x_ref, mod_ref, gn_ref, win_ref, wdw_ref, bdw_ref, lng_ref, lnb_ref, waout_ref, wgrp_ref,
    psc_ref, wbout_ref, wout_ref, gfin_ref, pwinv_ref, s_ref, ps_ref,
    y_ref, u_ref, pu_ref,
    xs,
):
    l = pl.program_id(0)

    @pl.when(l == 0)
    def _():
        xs[...] = x_ref[...]

    x = xs[...]
    mod = mod_ref[...]
    shift = mod[:, 0:D_MODEL]
    scale = mod[:, D_MODEL:2 * D_MODEL]
    gate = mod[:, 2 * D_MODEL:3 * D_MODEL]
    h = (_rms(x, gn_ref[...]) * (1.0 + scale) + shift).astype(BF16)
    proj = jnp.dot(h, _bf16_rows(win_ref[...]), preferred_element_type=F32)

    def col(j):
        return proj[:, _col(j)]

    u = col(0) * _sigmoid(col(1))
    u_ref[...] = u
    y = s_ref[...] + wdw_ref[CONV_BUF:CONV_WIDTH, :] * u + bdw_ref[...]
    a = _layer_norm_silu(y, lng_ref[...], lnb_ref[...]) * _silu(col(2))
    y_a = jnp.dot(a.astype(BF16), _bf16_rows(waout_ref[...]), preferred_element_type=F32)

    pu = col(3)
    pu_ref[...] = pu
    pm = ((ps_ref[...] + pu) * pwinv_ref[...] - pu).astype(BF16)
    q = jnp.concatenate(
        [jnp.dot(pm[:, g * POOL_GROUP:(g + 1) * POOL_GROUP], _bf16_rows(wgrp_ref[g]), preferred_element_type=F32)
         for g in range(len(POOL_WINDOWS))], axis=-1)
    bq = q * psc_ref[...] * _silu(col(4))
    y_b = jnp.dot(bq.astype(BF16), _bf16_rows(wbout_ref[...]), preferred_element_type=F32)

    m = _sigmoid(col(5)) * y_a + _sigmoid(col(6)) * y_b
    xn = x + gate * jnp.dot(m.astype(BF16), _bf16_rows(wout_ref[...]), preferred_element_type=F32)
    xs[...] = xn

    @pl.when(l == DEPTH - 1)
    def _():
        y_ref[...] = _rms(xn, gfin_ref[...])


def _sample_trunk(x, mod, p, pwinv, s_all, ps_all):
    nb = x.shape[0]

    def lspec(shape):
        nd = len(shape)
        return pl.BlockSpec((None,) + shape, lambda l: (l,) + (0,) * nd)

    def cspec(shape):
        nd = len(shape)
        return pl.BlockSpec(shape, lambda l: (0,) * nd)

    in_specs = [
        cspec((nb, D_MODEL)),
        lspec((nb, 3 * D_MODEL)),
        lspec((1, D_MODEL)),
        lspec((D_MODEL // 2, N_PROJ * D_MODEL)),
        lspec((CONV_WIDTH, D_MODEL)),
        lspec((1, D_MODEL)),
        lspec((1, D_MODEL)),
        lspec((1, D_MODEL)),
        lspec((D_MODEL // 2, D_MODEL)),
        lspec((len(POOL_WINDOWS), POOL_GROUP // 2, POOL_GROUP)),
        lspec((1, D_MODEL)),
        lspec((D_MODEL // 2, D_MODEL)),
        lspec((D_MODEL // 2, D_MODEL)),
        cspec((1, D_MODEL)),
        cspec((1, D_MODEL)),
        lspec((nb, D_MODEL)),
        lspec((nb, D_MODEL)),
    ]
    out_specs = [cspec((nb, D_MODEL)), lspec((nb, D_MODEL)), lspec((nb, D_MODEL))]
    out_shape = [
        jax.ShapeDtypeStruct((nb, D_MODEL), F32),
        jax.ShapeDtypeStruct((DEPTH, nb, D_MODEL), F32),
        jax.ShapeDtypeStruct((DEPTH, nb, D_MODEL), F32),
    ]
    return pl.pallas_call(
        _sample_trunk_kernel,
        grid=(DEPTH,),
        in_specs=in_specs,
        out_specs=out_specs,
        out_shape=out_shape,
        scratch_shapes=[pltpu.VMEM((nb, D_MODEL), F32)],
        compiler_params=pltpu.CompilerParams(
            dimension_semantics=("arbitrary",), vmem_limit_bytes=VMEM_LIMIT_BYTES),
        name="sample_trunk",
    )(x, mod, p["g_norm"], p["w_in"], p["w_dw"], p["b_dw"], p["ln_g"], p["ln_b"], p["w_a_out"],
      p["w_grp"], p["pool_scale"], p["w_b_out"], p["w_out"], p["g_final"], pwinv, s_all, ps_all)


def _state_shift_kernel(rows_ref, next_ref, new_ref, o_ref, *, n_blocks):
    o_ref[0:SHIFT_ROWS - 1] = rows_ref[1:SHIFT_ROWS]
    last = pl.program_id(1) == n_blocks - 1
    o_ref[SHIFT_ROWS - 1] = jnp.where(last, new_ref[...], next_ref[0])


def _state_shift(state_t, new_rows, name):
    _, nbuf, nb, _ = state_t.shape
    n_blocks = nbuf // SHIFT_ROWS
    return pl.pallas_call(
        functools.partial(_state_shift_kernel, n_blocks=n_blocks),
        grid=(DEPTH, n_blocks),
        in_specs=[
            pl.BlockSpec((None, SHIFT_ROWS, nb, D_MODEL), lambda l, i: (l, i, 0, 0)),
            pl.BlockSpec((None, 1, nb, D_MODEL),
                         lambda l, i: (l, jnp.minimum((i + 1) * SHIFT_ROWS, nbuf - 1), 0, 0)),
            pl.BlockSpec((None, nb, D_MODEL), lambda l, i: (l, 0, 0)),
        ],
        out_specs=pl.BlockSpec((None, SHIFT_ROWS, nb, D_MODEL), lambda l, i: (l, i, 0, 0)),
        out_shape=jax.ShapeDtypeStruct(state_t.shape, F32),
        name=name,
    )(state_t, state_t, new_rows)


def kernel(x_prompt, x_sample, state_conv, state_pool, c_prompt, c_sample, w_ada, b_ada, g_norm, w_in, w_dw, b_dw, ln_g, ln_b, w_a_out, w_grp, pool_scale, w_b_out, w_out, g_final):
    n_sample = x_sample.shape[0]
    row = lambda a: a.reshape(DEPTH, 1, D_MODEL)
    p = dict(
        g_norm=row(g_norm), w_in=_pack_rows(w_in, "pack_w_in"), w_dw=w_dw, b_dw=row(b_dw), ln_g=row(ln_g),
        ln_b=row(ln_b), w_a_out=_pack_rows(w_a_out, "pack_w_a_out"),
        w_grp=_pack_rows(w_grp.reshape(-1, POOL_GROUP, POOL_GROUP), "pack_w_grp").reshape(
            DEPTH, len(POOL_WINDOWS), POOL_GROUP // 2, POOL_GROUP),
        pool_scale=row(pool_scale), w_b_out=_pack_rows(w_b_out, "pack_w_b_out"),
        w_out=_pack_rows(w_out, "pack_w_out"),
        g_final=g_final.reshape(1, D_MODEL),
    )
    win_lane = jnp.repeat(jnp.asarray(POOL_WINDOWS, F32), POOL_GROUP).reshape(1, D_MODEL)

    mod = _ada_mod(jnp.concatenate([c_sample, c_prompt], axis=0), w_ada, b_ada)

    x = x_prompt
    conv_p, pool_p = [], []
    for l in range(DEPTH):
        x, cn, pn = _prompt_layer(l, x, mod, n_sample, p, final=(l == DEPTH - 1))
        conv_p.append(cn)
        pool_p.append(pn)
    y_prompt = x

    sc_t = jnp.transpose(state_conv, (0, 2, 1, 3))
    sp_t = jnp.transpose(state_pool, (0, 2, 1, 3))
    s_all, ps_all = _state_reduce(sc_t, sp_t, w_dw)
    y_s, u_all, pu_all = _sample_trunk(x_sample[:, 0, :], mod, p, 1.0 / win_lane, s_all, ps_all)
    conv_s = jnp.transpose(_state_shift(sc_t, u_all, "sample_conv_state"), (0, 2, 1, 3))
    pool_s = jnp.transpose(_state_shift(sp_t, pu_all, "sample_pool_state"), (0, 2, 1, 3))

    return (y_prompt, y_s[:, None, :], jnp.stack(conv_p), conv_s, jnp.stack(pool_p), pool_s)
```
